```python
import math
import jax, jax.numpy as jnp
from jax import lax
import numpy as np

D_MODEL = 2048
BATCH = 4
SEQ = 4096
DEPTH = 2
DEC_BATCH = 8
DEC_SEQ = 4096
PAST_LEN = 128

GRID_W = 64
POOL_GROUPS = 4
POOL_GROUP_DIM = 256
POOL_WIDTH = POOL_GROUPS * POOL_GROUP_DIM
POOL_WINDOWS = (2, 4, 8, 16)
MLA_HEADS = 16
MLA_Q_RANK = 512
MLA_KV_RANK = 512
MLA_NOPE = 128
MLA_ROPE = 64
MLA_QK = MLA_NOPE + MLA_ROPE
MLA_V = 128
MLA_WIDTH = MLA_HEADS * MLA_V
MLA_BLOCK = 128
ROPE_THETA = 10000.0
NA_HEADS = 16
NA_HEAD_DIM = 64
NA_WIDTH = NA_HEADS * NA_HEAD_DIM
NA_KH_MAX = 8
NA_KW = 16
N_BRANCH = 3
N_EXPERTS = 16
EXPERT_FF = 1408
EC_CAPACITY = 2
EPS = 1e-6

OFF_POOL = 0
OFF_CQ = OFF_POOL + POOL_WIDTH
OFF_CKV = OFF_CQ + MLA_Q_RANK
OFF_KR = OFF_CKV + MLA_KV_RANK
OFF_NA = OFF_KR + MLA_ROPE
OFF_GATE = OFF_NA + 3 * NA_WIDTH
IN_COLS = OFF_GATE + N_BRANCH * D_MODEL

kernel_name = 'hybrid_pool_mla_natten_ec_encoder'


def _rmsnorm(x, g):
    xf = x.astype(jnp.float32)
    y = xf * lax.rsqrt(jnp.mean(xf * xf, axis=-1, keepdims=True) + EPS)
    return (y * g.astype(jnp.float32)).astype(x.dtype)


def _pool_mixer(u, pool_w, pool_scale):
    B, T, _ = u.shape
    uf = u.reshape(B, T, POOL_GROUPS, POOL_GROUP_DIM).astype(jnp.float32)
    cs = jnp.concatenate([jnp.zeros((B, 1, POOL_GROUPS, POOL_GROUP_DIM), jnp.float32), jnp.cumsum(uf, axis=1)], axis=1)
    t = np.arange(T)
    outs = []
    for g, w in enumerate(POOL_WINDOWS):
        left = w // 2
        right = w - 1 - left
        lo = np.maximum(t - left, 0)
        hi = np.minimum(t + right, T - 1) + 1
        cnt = jnp.asarray((hi - lo).astype(np.float32))
        s = cs[:, hi, g] - cs[:, lo, g]
        outs.append(s / cnt[None, :, None] - uf[:, :, g])
    pooled = jnp.stack(outs, axis=2).astype(u.dtype)
    mixed = jnp.einsum('btgc,gcd->btgd', pooled, pool_w).reshape(B, T, POOL_WIDTH)
    return mixed * pool_scale


def _rope_tail(x, cos, sin):
    half = MLA_ROPE // 2
    xf = x.astype(jnp.float32)
    nope = xf[..., :MLA_NOPE]
    r1 = xf[..., MLA_NOPE:MLA_NOPE + half]
    r2 = xf[..., MLA_NOPE + half:]
    return jnp.concatenate([nope, r1 * cos - r2 * sin, r2 * cos + r1 * sin], axis=-1).astype(x.dtype)


def _dense_attn_blocks(q, k, v):
    B, T, H, Dq = q.shape
    nb = T // MLA_BLOCK
    scale = Dq ** -0.5
    qb = q.reshape(B, nb, MLA_BLOCK, H, Dq).transpose(1, 0, 2, 3, 4)

    def blk(qi):
        s = jnp.einsum('bqhd,bkhd->bhqk', qi, k).astype(jnp.float32) * scale
        p = jax.nn.softmax(s, axis=-1).astype(v.dtype)
        return jnp.einsum('bhqk,bkhd->bqhd', p, v)

    o = lax.map(blk, qb)
    return o.transpose(1, 0, 2, 3, 4).reshape(B, T, H * v.shape[-1])


def _neighbourhood_attn(q, k, v, rel_bias):
    B, T, H, Dh = q.shape
    rows = T // GRID_W
    kh = min(NA_KH_MAX, rows)
    kw = NA_KW
    scale = Dh ** -0.5
    qg = q.reshape(B, rows, GRID_W, H, Dh)
    kg = k.reshape(B, rows, GRID_W, H, Dh)
    vg = v.reshape(B, rows, GRID_W, H, Dh)
    cols = np.arange(GRID_W)
    sc = np.clip(cols - kw // 2, 0, GRID_W - kw)
    col_idx = sc[:, None] + np.arange(kw)[None, :]
    dx_idx = col_idx - cols[:, None] + (kw - 1)

    def row(r):
        sr = jnp.clip(r - kh // 2, 0, rows - kh)
        qr = lax.dynamic_index_in_dim(qg, r, axis=1, keepdims=False)
        kr = lax.dynamic_slice_in_dim(kg, sr, kh, axis=1)
        vr = lax.dynamic_slice_in_dim(vg, sr, kh, axis=1)
        kwin = kr[:, :, col_idx]
        vwin = vr[:, :, col_idx]
        dy_idx = sr + jnp.arange(kh) - r + (NA_KH_MAX - 1)
        bias = rel_bias[:, dy_idx[None, :, None], dx_idx[:, None, :]]
        s = jnp.einsum('bqhd,biqjhd->bhqij', qr, kwin).astype(jnp.float32) * scale + bias.astype(jnp.float32)[None]
        p = jax.nn.softmax(s.reshape(B, H, GRID_W, kh * kw), axis=-1).reshape(B, H, GRID_W, kh, kw).astype(v.dtype)
        return jnp.einsum('bhqij,biqjhd->bqhd', p, vwin)

    o = lax.map(row, jnp.arange(rows))
    return o.transpose(1, 0, 2, 3, 4).reshape(B, T, H * Dh)


def _expert_choice_ffn(h, w_router, w_e_gate, w_e_up, w_e_down):
    B, T, D = h.shape
    N = B * T
    cap = EC_CAPACITY * N // N_EXPERTS
    x2d = h.reshape(N, D)
    aff = jax.nn.softmax(jnp.dot(x2d, w_router).astype(jnp.float32), axis=-1)
    gate, idx = lax.top_k(aff.T, cap)
    xe = x2d[idx]
    hid = jax.nn.silu(jnp.einsum('ecd,edf->ecf', xe, w_e_gate)) * jnp.einsum('ecd,edf->ecf', xe, w_e_up)
    ye = jnp.einsum('ecf,efd->ecd', hid, w_e_down) * gate[..., None].astype(h.dtype)
    y = jnp.zeros_like(x2d).at[idx.reshape(-1)].add(ye.reshape(-1, D))
    return y.reshape(B, T, D)


def _trunk(x, g_mix, w_in, pool_w, pool_scale, mla_gq, mla_gkv, mla_w_uq, mla_w_ukv, mla_q_norm, mla_k_norm,
           na_q_norm, na_k_norm, na_rel_bias, b_gate, w_br_pool, w_br_mla, w_br_na, w_out, g_ffn, w_router,
           w_e_gate, w_e_up, w_e_down):
    B, T, D = x.shape
    pos = jnp.arange(T, dtype=jnp.float32)
    inv = 1.0 / (ROPE_THETA ** (jnp.arange(0, MLA_ROPE, 2, dtype=jnp.float32) / MLA_ROPE))
    ang = pos[:, None] * inv[None, :]
    cos = jnp.cos(ang)[:, None, :]
    sin = jnp.sin(ang)[:, None, :]
    for l in range(DEPTH):
        h = _rmsnorm(x, g_mix[l])
        proj = jnp.dot(h, w_in[l])
        br_pool = jnp.dot(_pool_mixer(proj[..., OFF_POOL:OFF_CQ], pool_w[l], pool_scale[l]), w_br_pool[l])
        cq = _rmsnorm(proj[..., OFF_CQ:OFF_CKV], mla_gq[l])
        ckv = _rmsnorm(proj[..., OFF_CKV:OFF_KR], mla_gkv[l])
        k_rope = proj[..., OFF_KR:OFF_NA]
        q = jnp.dot(cq, mla_w_uq[l]).reshape(B, T, MLA_HEADS, MLA_QK)
        kv = jnp.dot(ckv, mla_w_ukv[l]).reshape(B, T, MLA_HEADS, MLA_NOPE + MLA_V)
        k = jnp.concatenate([kv[..., :MLA_NOPE], jnp.broadcast_to(k_rope[:, :, None, :], (B, T, MLA_HEADS, MLA_ROPE))], axis=-1)
        v = kv[..., MLA_NOPE:]
        q = _rope_tail(_rmsnorm(q, mla_q_norm[l]), cos, sin)
        k = _rope_tail(_rmsnorm(k, mla_k_norm[l]), cos, sin)
        br_mla = jnp.dot(_dense_attn_blocks(q, k, v), w_br_mla[l])
        qkv = proj[..., OFF_NA:OFF_GATE].reshape(B, T, 3, NA_HEADS, NA_HEAD_DIM)
        nq = _rmsnorm(qkv[:, :, 0], na_q_norm[l])
        nk = _rmsnorm(qkv[:, :, 1], na_k_norm[l])
        br_na = jnp.dot(_neighbourhood_attn(nq, nk, qkv[:, :, 2], na_rel_bias[l]), w_br_na[l])
        gp = proj[..., OFF_GATE:].reshape(B, T, N_BRANCH, D) + b_gate[l]
        gates = jax.nn.sigmoid(gp.astype(jnp.float32)).astype(x.dtype)
        merged = gates[:, :, 0] * br_pool + gates[:, :, 1] * br_mla + gates[:, :, 2] * br_na
        x = x + jnp.dot(merged, w_out[l])
        h2 = _rmsnorm(x, g_ffn[l])
        x = x + _expert_choice_ffn(h2, w_router[l], w_e_gate[l], w_e_up[l], w_e_down[l])
    return x


def setup_inputs(seed: int = 0) -> dict:
    key = jax.random.key(seed)
    ks = jax.random.split(key, 26)
    f32 = jnp.float32
    L = DEPTH

    def nrm(k, shape, scale):
        return jax.random.normal(k, shape, f32) * scale

    return {
        'x_prompt': nrm(ks[0], (BATCH, SEQ, D_MODEL), 1.0),
        'x_sample': nrm(ks[1], (DEC_BATCH, DEC_SEQ, D_MODEL), 1.0),
        'g_mix': 1.0 + nrm(ks[2], (L, D_MODEL), 0.02),
        'w_in': nrm(ks[3], (L, D_MODEL, IN_COLS), D_MODEL ** -0.5),
        'pool_w': nrm(ks[4], (L, POOL_GROUPS, POOL_GROUP_DIM, POOL_GROUP_DIM), POOL_GROUP_DIM ** -0.5),
        'pool_scale': 1.0 + nrm(ks[5], (L, POOL_WIDTH), 0.02),
        'mla_gq': 1.0 + nrm(ks[6], (L, MLA_Q_RANK), 0.02),
        'mla_gkv': 1.0 + nrm(ks[7], (L, MLA_KV_RANK), 0.02),
        'mla_w_uq': nrm(ks[8], (L, MLA_Q_RANK, MLA_HEADS * MLA_QK), MLA_Q_RANK ** -0.5),
        'mla_w_ukv': nrm(ks[9], (L, MLA_KV_RANK, MLA_HEADS * (MLA_NOPE + MLA_V)), MLA_KV_RANK ** -0.5),
        'mla_q_norm': 1.0 + nrm(ks[10], (L, MLA_QK), 0.02),
        'mla_k_norm': 1.0 + nrm(ks[11], (L, MLA_QK), 0.02),
        'na_q_norm': 1.0 + nrm(ks[12], (L, NA_HEAD_DIM), 0.02),
        'na_k_norm': 1.0 + nrm(ks[13], (L, NA_HEAD_DIM), 0.02),
        'na_rel_bias': nrm(ks[14], (L, NA_HEADS, 2 * NA_KH_MAX - 1, 2 * NA_KW - 1), 0.1),
        'b_gate': nrm(ks[15], (L, N_BRANCH, D_MODEL), 0.02),
        'w_br_pool': nrm(ks[16], (L, POOL_WIDTH, D_MODEL), POOL_WIDTH ** -0.5),
        'w_br_mla': nrm(ks[17], (L, MLA_WIDTH, D_MODEL), MLA_WIDTH ** -0.5),
        'w_br_na': nrm(ks[18], (L, NA_WIDTH, D_MODEL), NA_WIDTH ** -0.5),
        'w_out': nrm(ks[19], (L, D_MODEL, D_MODEL), D_MODEL ** -0.5),
        'g_ffn': 1.0 + nrm(ks[20], (L, D_MODEL), 0.02),
        'w_router': nrm(ks[21], (L, D_MODEL, N_EXPERTS), D_MODEL ** -0.5),
        'w_e_gate': nrm(ks[22], (L, N_EXPERTS, D_MODEL, EXPERT_FF), D_MODEL ** -0.5),
        'w_e_up': nrm(ks[23], (L, N_EXPERTS, D_MODEL, EXPERT_FF), D_MODEL ** -0.5),
        'w_e_down': nrm(ks[24], (L, N_EXPERTS, EXPERT_FF, D_MODEL), EXPERT_FF ** -0.5),
    }


def reference(x_prompt, x_sample, g_mix, w_in, pool_w, pool_scale, mla_gq, mla_gkv, mla_w_uq, mla_w_ukv,
              mla_q_norm, mla_k_norm, na_q_norm, na_k_norm, na_rel_bias, b_gate, w_br_pool, w_br_mla, w_br_na,
              w_out, g_ffn, w_router, w_e_gate, w_e_up, w_e_down):
    weights = (g_mix, w_in, pool_w, pool_scale, mla_gq, mla_gkv, mla_w_uq, mla_w_ukv, mla_q_norm, mla_k_norm,
               na_q_norm, na_k_norm, na_rel_bias, b_gate, w_br_pool, w_br_mla, w_br_na, w_out, g_ffn, w_router,
               w_e_gate, w_e_up, w_e_down)
    y_prompt = _trunk(x_prompt, *weights)
    y_sample = _trunk(x_sample, *weights)
    return (y_prompt, y_sample)
```

```python
import functools
import math

import numpy as np
import jax
import jax.numpy as jnp
from jax import lax
from jax.experimental import pallas as pl
from jax.experimental.pallas import tpu as pltpu

F32 = jnp.float32
BF16 = jnp.bfloat16
I32 = jnp.int32

LANES = 128
VMEM_LIMIT = 56 * 1024 * 1024

GRID_W = 64
POOL_GROUPS = 4
POOL_GROUP_DIM = 256
POOL_WIDTH = POOL_GROUPS * POOL_GROUP_DIM
POOL_WINDOWS = (2, 4, 8, 16)
POOL_HALO = 8
MLA_HEADS = 16
MLA_RANK = 512
MLA_NOPE = 128
MLA_ROPE = 64
MLA_QK = MLA_NOPE + MLA_ROPE
MLA_QK_PAD = 256
MLA_V = 128
MLA_SEG = 2 * MLA_RANK + LANES
ROPE_THETA = 10000.0
NA_HEADS = 16
NA_HEAD_DIM = 64
NA_WIDTH = NA_HEADS * NA_HEAD_DIM
NA_KH = 8
NA_KW = 16
NA_QROWS = 8
NA_KROWS = 16
N_BRANCH = 3
N_EXPERTS = 16
EC_CAPACITY = 2
EPS = 1e-6
NEG = -1e30


def _cparams(*sem):
    return pltpu.CompilerParams(dimension_semantics=sem, vmem_limit_bytes=VMEM_LIMIT)


def _rmsnorm_kernel(x_ref, g_ref, o_ref):
    x = x_ref[...]
    y = x * lax.rsqrt(jnp.mean(x * x, axis=-1, keepdims=True) + EPS) * g_ref[...]
    o_ref[...] = y.astype(o_ref.dtype)


def _rmsnorm(x, g, tm=512):
    m, d = x.shape
    return pl.pallas_call(
        _rmsnorm_kernel,
        grid=(m // tm,),
        in_specs=[pl.BlockSpec((tm, d), lambda i: (i, 0)), pl.BlockSpec((1, d), lambda i: (0, 0))],
        out_specs=pl.BlockSpec((tm, d), lambda i: (i, 0)),
        out_shape=jax.ShapeDtypeStruct((m, d), BF16),
        compiler_params=_cparams("parallel"),
    )(x, g.reshape(1, d))


def _norm_router_kernel(x_ref, g_ref, wr_ref, h_ref, aff_ref):
    x = x_ref[...]
    y = x * lax.rsqrt(jnp.mean(x * x, axis=-1, keepdims=True) + EPS) * g_ref[...]
    h_ref[...] = y
    logits = lax.dot_general(wr_ref[...], y, (((1,), (1,)), ((), ())), precision=lax.Precision.HIGHEST,
                             preferred_element_type=F32)
    mx = jnp.max(logits, axis=0, keepdims=True)
    ex = jnp.exp(logits - mx)
    aff_ref[...] = ex / jnp.sum(ex, axis=0, keepdims=True)


def _norm_router(x, g, w_router_t, tm=512):
    m, d = x.shape
    e = w_router_t.shape[0]
    return pl.pallas_call(
        _norm_router_kernel,
        grid=(m // tm,),
        in_specs=[pl.BlockSpec((tm, d), lambda i: (i, 0)), pl.BlockSpec((1, d), lambda i: (0, 0)),
                  pl.BlockSpec((e, d), lambda i: (0, 0))],
        out_specs=[pl.BlockSpec((tm, d), lambda i: (i, 0)), pl.BlockSpec((e, tm), lambda i: (0, i))],
        out_shape=[jax.ShapeDtypeStruct((m, d), F32), jax.ShapeDtypeStruct((e, m), F32)],
        compiler_params=_cparams("parallel"),
    )(x, g.reshape(1, d), w_router_t)


def _mm_kernel(a_ref, w_ref, *rest, epilogue):
    acc = jnp.dot(a_ref[...], w_ref[...], preferred_element_type=F32)
    if epilogue == "sigmoid_bias":
        b_ref, o_ref = rest
        acc = jax.nn.sigmoid(acc + b_ref[...])
    elif epilogue == "residual":
        r_ref, o_ref = rest
        acc = acc + r_ref[...]
    else:
        (o_ref,) = rest
    o_ref[...] = acc.astype(o_ref.dtype)


def _matmul(a, w, out_dtype, tn, tm=512, bias=None, residual=None):
    m, k = a.shape
    n = w.shape[1]
    in_specs = [pl.BlockSpec((tm, k), lambda j, i: (i, 0)), pl.BlockSpec((k, tn), lambda j, i: (0, j))]
    args = [a, w]
    epilogue = "none"
    if bias is not None:
        epilogue = "sigmoid_bias"
        in_specs.append(pl.BlockSpec((1, tn), lambda j, i: (0, j)))
        args.append(bias.reshape(1, n))
    if residual is not None:
        epilogue = "residual"
        in_specs.append(pl.BlockSpec((tm, tn), lambda j, i: (i, j)))
        args.append(residual)
    return pl.pallas_call(
        functools.partial(_mm_kernel, epilogue=epilogue),
        grid=(n // tn, m // tm),
        in_specs=in_specs,
        out_specs=pl.BlockSpec((tm, tn), lambda j, i: (i, j)),
        out_shape=jax.ShapeDtypeStruct((m, n), out_dtype),
        compiler_params=_cparams("parallel", "parallel"),
    )(*args)


def _pool_kernel(u_ref, w_ref, s_ref, o_ref, pad_ref, *, seq, chunk):
    g = pl.program_id(1)
    zeros = jnp.zeros((POOL_HALO, POOL_GROUP_DIM), F32)
    pad_ref[0:POOL_HALO, :] = zeros
    pad_ref[POOL_HALO + seq:2 * POOL_HALO + seq, :] = zeros
    pad_ref[POOL_HALO:POOL_HALO + seq, :] = u_ref[0]

    def window(w):
        left = w // 2
        right = w - 1 - left
        for c in range(seq // chunk):
            base = POOL_HALO + c * chunk
            acc = pad_ref[base - left:base - left + chunk, :]
            for d in range(-left + 1, right + 1):
                acc = acc + pad_ref[base + d:base + d + chunk, :]
            t = c * chunk + lax.broadcasted_iota(I32, (chunk, 1), 0)
            cnt = (jnp.minimum(t + right, seq - 1) + 1 - jnp.maximum(t - left, 0)).astype(F32)
            pooled = acc / cnt - pad_ref[base:base + chunk, :]
            mixed = jnp.dot(pooled.astype(BF16), w_ref[0], preferred_element_type=F32) * s_ref[...]
            o_ref[0, c * chunk:(c + 1) * chunk, :] = mixed.astype(o_ref.dtype)

    for k, w in enumerate(POOL_WINDOWS):
        pl.when(g == k)(functools.partial(window, w))


def _pool_mixer(u, pool_w, pool_scale):
    b, t, _ = u.shape
    chunk = min(512, t)
    return pl.pallas_call(
        functools.partial(_pool_kernel, seq=t, chunk=chunk),
        grid=(b, POOL_GROUPS),
        in_specs=[pl.BlockSpec((1, t, POOL_GROUP_DIM), lambda i, g: (i, 0, g)),
                  pl.BlockSpec((1, POOL_GROUP_DIM, POOL_GROUP_DIM), lambda i, g: (g, 0, 0)),
                  pl.BlockSpec((1, POOL_GROUP_DIM), lambda i, g: (0, g))],
        out_specs=pl.BlockSpec((1, t, POOL_GROUP_DIM), lambda i, g: (i, 0, g)),
        out_shape=jax.ShapeDtypeStruct((b, t, POOL_WIDTH), BF16),
        scratch_shapes=[pltpu.VMEM((t + 2 * POOL_HALO, POOL_GROUP_DIM), F32)],
        compiler_params=_cparams("parallel", "parallel"),
    )(u, pool_w, pool_scale.reshape(1, POOL_WIDTH))


def _mla_prep_kernel(seg_ref, gq_ref, gkv_ref, wuq_ref, wukv_ref, qn_ref, kn_ref, cos_ref, sin_ref,
                     q_ref, k_ref, v_ref):
    def rms(x):
        return x * lax.rsqrt(jnp.mean(x * x, axis=-1, keepdims=True) + EPS)

    def rope(x):
        lane = lax.broadcasted_iota(I32, x.shape, 1)
        partner = jnp.where(lane < MLA_ROPE // 2, pltpu.roll(x, LANES - MLA_ROPE // 2, 1),
                            pltpu.roll(x, MLA_ROPE // 2, 1))
        return x * cos_ref[...] + partner * sin_ref[...]

    cq = (rms(seg_ref[:, 0:MLA_RANK]) * gq_ref[...]).astype(BF16)
    ckv = (rms(seg_ref[:, MLA_RANK:2 * MLA_RANK]) * gkv_ref[...]).astype(BF16)
    kr = seg_ref[:, 2 * MLA_RANK:2 * MLA_RANK + LANES]
    q = jnp.dot(cq, wuq_ref[...], preferred_element_type=F32)
    kv = jnp.dot(ckv, wukv_ref[...], preferred_element_type=F32)
    hn = MLA_HEADS * MLA_NOPE
    kr_sq = kr * kr
    kr_rot = rope(kr * kn_ref[:, LANES:2 * LANES])
    inv_d = 1.0 / MLA_QK
    for h in range(MLA_HEADS):
        qa = q[:, h * LANES:(h + 1) * LANES]
        qb = q[:, hn + h * LANES:hn + (h + 1) * LANES]
        rs = lax.rsqrt(jnp.sum(qa * qa + qb * qb, axis=-1, keepdims=True) * inv_d + EPS)
        q_ref[0, h, :, 0:LANES] = (qa * rs * qn_ref[:, 0:LANES]).astype(BF16)
        q_ref[0, h, :, LANES:2 * LANES] = rope(qb * rs * qn_ref[:, LANES:2 * LANES]).astype(BF16)
        ka = kv[:, h * LANES:(h + 1) * LANES]
        rs = lax.rsqrt(jnp.sum(ka * ka + kr_sq, axis=-1, keepdims=True) * inv_d + EPS)
        k_ref[0, h, :, 0:LANES] = (ka * rs * kn_ref[:, 0:LANES]).astype(BF16)
        k_ref[0, h, :, LANES:2 * LANES] = (kr_rot * rs).astype(BF16)
        v_ref[0, h, :, :] = kv[:, hn + h * LANES:hn + (h + 1) * LANES].astype(BF16)


def _mla_prep(seg, gq, gkv, wuq, wukv, qn, kn, cos_t, sin_t, b, t, tm=256):
    m = b * t
    tpb = t // tm
    const = lambda i: (0, 0)
    head_map = lambda i: (i // tpb, 0, i % tpb, 0)
    return pl.pallas_call(
        _mla_prep_kernel,
        grid=(m // tm,),
        in_specs=[pl.BlockSpec((tm, MLA_SEG), lambda i: (i, 0)),
                  pl.BlockSpec((1, MLA_RANK), const), pl.BlockSpec((1, MLA_RANK), const),
                  pl.BlockSpec(wuq.shape, const), pl.BlockSpec(wukv.shape, const),
                  pl.BlockSpec((1, 2 * LANES), const), pl.BlockSpec((1, 2 * LANES), const),
                  pl.BlockSpec((tm, LANES), lambda i: (i % tpb, 0)), pl.BlockSpec((tm, LANES), lambda i: (i % tpb, 0))],
        out_specs=[pl.BlockSpec((1, MLA_HEADS, tm, MLA_QK_PAD), head_map),
                   pl.BlockSpec((1, MLA_HEADS, tm, MLA_QK_PAD), head_map),
                   pl.BlockSpec((1, MLA_HEADS, tm, MLA_V), head_map)],
        out_shape=[jax.ShapeDtypeStruct((b, MLA_HEADS, t, MLA_QK_PAD), BF16),
                   jax.ShapeDtypeStruct((b, MLA_HEADS, t, MLA_QK_PAD), BF16),
                   jax.ShapeDtypeStruct((b, MLA_HEADS, t, MLA_V), BF16)],
        compiler_params=_cparams("parallel"),
    )(seg, gq, gkv, wuq, wukv, qn, kn, cos_t, sin_t)


def _mla_attn_kernel(q_ref, k_ref, v_ref, o_ref, *, tk):
    q = q_ref[0, 0]
    tq = q.shape[0]
    nk = k_ref.shape[2] // tk

    def body(c, carry):
        m, l, acc = carry
        k = k_ref[0, 0, pl.ds(pl.multiple_of(c * tk, tk), tk), :]
        v = v_ref[0, 0, pl.ds(pl.multiple_of(c * tk, tk), tk), :]
        s = lax.dot_general(q, k, (((1,), (1,)), ((), ())), preferred_element_type=F32)
        m_new = jnp.maximum(m, jnp.max(s, axis=-1, keepdims=True))
        alpha = jnp.exp(m - m_new)
        p = jnp.exp(s - m_new)
        l = alpha * l + jnp.sum(p, axis=-1, keepdims=True)
        acc = alpha * acc + jnp.dot(p.astype(BF16), v, preferred_element_type=F32)
        return m_new, l, acc

    m0 = jnp.full((tq, 1), NEG, F32)
    l0 = jnp.zeros((tq, 1), F32)
    a0 = jnp.zeros((tq, MLA_V), F32)
    _, l, acc = lax.fori_loop(0, nk, body, (m0, l0, a0))
    o_ref[0] = (acc / l).astype(o_ref.dtype)


def _mla_attn(q, k, v, tq=256, tk=512):
    b, h, t, _ = q.shape
    tk = min(tk, t)
    return pl.pallas_call(
        functools.partial(_mla_attn_kernel, tk=tk),
        grid=(b, h, t // tq),
        in_specs=[pl.BlockSpec((1, 1, tq, MLA_QK_PAD), lambda i, j, n: (i, j, n, 0)),
                  pl.BlockSpec((1, 1, t, MLA_QK_PAD), lambda i, j, n: (i, j, 0, 0)),
                  pl.BlockSpec((1, 1, t, MLA_V), lambda i, j, n: (i, j, 0, 0))],
        out_specs=pl.BlockSpec((1, tq, MLA_V), lambda i, j, n: (i, n, j)),
        out_shape=jax.ShapeDtypeStruct((b, t, h * MLA_V), BF16),
        compiler_params=_cparams("parallel", "parallel", "parallel"),
    )(q, k, v)


def _na_bias_tables(rel_bias, rows):
    kw = NA_KW
    qr = np.arange(NA_QROWS)[:, None, None, None]
    qc = np.arange(GRID_W)[None, :, None, None]
    kr = np.arange(NA_KROWS)[None, None, :, None]
    kc = np.arange(GRID_W)[None, None, None, :]
    sc = np.clip(qc - kw // 2, 0, GRID_W - kw)
    tables = []
    for r0 in (0, NA_QROWS, rows - NA_QROWS):
        ks = int(np.clip(r0 - NA_KH // 2, 0, rows - NA_KROWS))
        qrow = r0 + qr
        krow = ks + kr
        sr = np.clip(qrow - NA_KH // 2, 0, rows - NA_KH)
        valid = (krow >= sr) & (krow < sr + NA_KH) & (kc >= sc) & (kc < sc + kw)
        dy = np.clip(krow - qrow + NA_KH - 1, 0, 2 * NA_KH - 2)
        dx = np.clip(kc - qc + kw - 1, 0, 2 * kw - 2)
        shape = (NA_QROWS * GRID_W, NA_KROWS * GRID_W)
        valid = np.broadcast_to(valid, (NA_QROWS, GRID_W, NA_KROWS, GRID_W)).reshape(shape)
        dy = np.broadcast_to(dy, (NA_QROWS, GRID_W, NA_KROWS, GRID_W)).reshape(shape)
        dx = np.broadcast_to(dx, (NA_QROWS, GRID_W, NA_KROWS, GRID_W)).reshape(shape)
        tables.append(jnp.where(valid[None], rel_bias[:, dy, dx], NEG))
    return jnp.stack(tables, axis=1).astype(F32)


def _na_kernel(q_ref, k_ref, v_ref, qn_ref, kn_ref, bias_ref, o_ref, kn_s, v_s, *, rows, chunk):
    rb = pl.program_id(2)
    nrb = pl.num_programs(2)
    seq = rows * GRID_W
    inv_d = 1.0 / NA_HEAD_DIM

    def head_norm(x, g):
        lane = lax.broadcasted_iota(I32, x.shape, 1)
        lo = lane < NA_HEAD_DIM
        sq = x * x
        s0 = jnp.sum(jnp.where(lo, sq, 0.0), axis=-1, keepdims=True)
        s1 = jnp.sum(jnp.where(lo, 0.0, sq), axis=-1, keepdims=True)
        rs = lax.rsqrt(jnp.where(lo, s0, s1) * inv_d + EPS)
        return x * rs * g

    @pl.when(rb == 0)
    def _():
        for c in range(seq // chunk):
            sl = slice(c * chunk, (c + 1) * chunk)
            kn_s[sl, :] = head_norm(k_ref[0, sl, :], kn_ref[...]).astype(BF16)
            v_s[sl, :] = v_ref[0, sl, :].astype(BF16)

    qn = head_norm(q_ref[0], qn_ref[...])
    ks = jnp.clip(rb * NA_QROWS - NA_KH // 2, 0, rows - NA_KROWS)
    start = pl.multiple_of(ks * GRID_W, GRID_W)
    kwin = kn_s[pl.ds(start, NA_KROWS * GRID_W), :]
    vwin = v_s[pl.ds(start, NA_KROWS * GRID_W), :]
    ty = jnp.where(rb == 0, 0, jnp.where(rb == nrb - 1, 2, 1))
    lane = lax.broadcasted_iota(I32, qn.shape, 1)
    outs = []
    for a in range(2):
        mine = lane < NA_HEAD_DIM if a == 0 else lane >= NA_HEAD_DIM
        qa = jnp.where(mine, qn, 0.0).astype(BF16)
        s = lax.dot_general(qa, kwin, (((1,), (1,)), ((), ())), preferred_element_type=F32)
        s = s + bias_ref[a, ty]
        m = jnp.max(s, axis=-1, keepdims=True)
        p = jnp.exp(s - m)
        l = jnp.sum(p, axis=-1, keepdims=True)
        outs.append(jnp.dot(p.astype(BF16), vwin, preferred_element_type=F32) / l)
    o_ref[0] = jnp.where(lane < NA_HEAD_DIM, outs[0], outs[1]).astype(o_ref.dtype)


def _na_attn(qkv, qn, kn, bias_tables, b, t):
    rows = t // GRID_W
    nrb = rows // NA_QROWS
    nhp = NA_WIDTH // LANES
    tq = NA_QROWS * GRID_W
    tkw = NA_KROWS * GRID_W
    return pl.pallas_call(
        functools.partial(_na_kernel, rows=rows, chunk=min(512, t)),
        grid=(nhp, b, nrb),
        in_specs=[pl.BlockSpec((1, tq, LANES), lambda hp, i, r: (i, r, hp)),
                  pl.BlockSpec((1, t, LANES), lambda hp, i, r: (i, 0, nhp + hp)),
                  pl.BlockSpec((1, t, LANES), lambda hp, i, r: (i, 0, 2 * nhp + hp)),
                  pl.BlockSpec((1, LANES), lambda hp, i, r: (0, 0)),
                  pl.BlockSpec((1, LANES), lambda hp, i, r: (0, 0)),
                  pl.BlockSpec((2, 3, tq, tkw), lambda hp, i, r: (hp, 0, 0, 0))],
        out_specs=pl.BlockSpec((1, tq, LANES), lambda hp, i, r: (i, r, hp)),
        out_shape=jax.ShapeDtypeStruct((b, t, NA_WIDTH), BF16),
        scratch_shapes=[pltpu.VMEM((t, LANES), BF16), pltpu.VMEM((t, LANES), BF16)],
        compiler_params=_cparams("parallel", "parallel", "arbitrary"),
    )(qkv, qkv, qkv, qn, kn, bias_tables)


def _merge_kernel(ap_ref, am_ref, an_ref, wp_ref, wm_ref, wn_ref, g0_ref, g1_ref, g2_ref, o_ref):
    acc = g0_ref[...].astype(F32) * jnp.dot(ap_ref[...], wp_ref[...], preferred_element_type=F32)
    acc = acc + g1_ref[...].astype(F32) * jnp.dot(am_ref[...], wm_ref[...], preferred_element_type=F32)
    acc = acc + g2_ref[...].astype(F32) * jnp.dot(an_ref[...], wn_ref[...], preferred_element_type=F32)
    o_ref[...] = acc.astype(o_ref.dtype)


def _merge(a_pool, a_mla, a_na, w_pool, w_mla, w_na, gates, tm=512, tn=512):
    m = a_pool.shape[0]
    d = w_pool.shape[1]
    nj = d // tn
    a_spec = lambda a: pl.BlockSpec((tm, a.shape[1]), lambda j, i: (i, 0))
    w_spec = lambda w: pl.BlockSpec((w.shape[0], tn), lambda j, i: (0, j))
    g_spec = lambda br: pl.BlockSpec((tm, tn), lambda j, i: (i, br * nj + j))
    return pl.pallas_call(
        _merge_kernel,
        grid=(nj, m // tm),
        in_specs=[a_spec(a_pool), a_spec(a_mla), a_spec(a_na), w_spec(w_pool), w_spec(w_mla), w_spec(w_na),
                  g_spec(0), g_spec(1), g_spec(2)],
        out_specs=pl.BlockSpec((tm, tn), lambda j, i: (i, j)),
        out_shape=jax.ShapeDtypeStruct((m, d), BF16),
        compiler_params=_cparams("parallel", "parallel"),
    )(a_pool, a_mla, a_na, w_pool, w_mla, w_na, gates, gates, gates)


def _route_kernel(aff_ref, pos_ref, idx_ref, off_ref, *, cap, idx_chunk):
    a = aff_ref[0]
    r = a.shape[0]
    n = r * LANES
    bits = lax.bitcast_convert_type(a, I32)
    tok = lax.broadcasted_iota(I32, (r, LANES), 0) * LANES + lax.broadcasted_iota(I32, (r, LANES), 1)

    def count(mask):
        return jnp.sum(jnp.sum(mask.astype(F32), axis=1, keepdims=True), axis=0, keepdims=True)

    thr = jnp.zeros((1, 1), I32)
    for bit in range(30, -1, -1):
        cand = thr | (1 << bit)
        thr = jnp.where(count(bits >= cand) >= cap, cand, thr)
    above = bits > thr
    tied = bits == thr
    need = cap - count(above)
    last = jnp.zeros((1, 1), I32)
    for bit in range(int(math.log2(n)) - 1, -1, -1):
        cand = last | (1 << bit)
        last = jnp.where(count(tied & (tok < cand)) < need, cand, last)
    sel = above | (tied & (tok <= last))
    m = sel.astype(BF16)

    li = lax.broadcasted_iota(I32, (LANES, LANES), 0)
    lj = lax.broadcasted_iota(I32, (LANES, LANES), 1)
    upper = (li <= lj).astype(BF16)
    c1 = jnp.dot(m, upper, preferred_element_type=F32)
    rowtot = jnp.broadcast_to(c1[:, LANES - 1:LANES], (r, LANES))
    ri = lax.broadcasted_iota(I32, (r, r), 0)
    rj = lax.broadcasted_iota(I32, (r, r), 1)
    rowoff = jnp.dot((rj < ri).astype(BF16), rowtot.astype(BF16), preferred_element_type=F32)
    pos_ref[0] = jnp.where(sel, c1 - 1.0 + rowoff, -1.0).astype(I32)
    off_ref[0] = rowoff.astype(I32)

    tot_l = lax.dot_general(jnp.ones((8, LANES), BF16), m, (((1,), (1,)), ((), ())),
                            preferred_element_type=F32)
    cum_l = jnp.dot(tot_l.astype(BF16), (ri <= rj).astype(BF16), preferred_element_type=F32)[0:1, :]
    excl_l = cum_l - tot_l[0:1, :]
    lane_r = lax.broadcasted_iota(I32, (idx_chunk, r), 1)
    for c in range(cap // idx_chunk):
        p = (c * idx_chunk + lax.broadcasted_iota(I32, (idx_chunk, 1), 0)).astype(F32)
        row = jnp.sum((cum_l <= p).astype(F32), axis=1, keepdims=True)
        onehot = lane_r == row.astype(I32)
        off = jnp.sum(jnp.where(onehot, excl_l, 0.0), axis=1, keepdims=True)
        mrow = jnp.dot(onehot.astype(BF16), m, preferred_element_type=F32)
        crow = jnp.dot(mrow.astype(BF16), upper, preferred_element_type=F32)
        lane = jnp.sum((crow <= p - off).astype(F32), axis=1, keepdims=True)
        idx_ref[0, c * idx_chunk:(c + 1) * idx_chunk, :] = (row * LANES + lane).astype(I32)


def _route(aff3, cap):
    e, r, _ = aff3.shape
    idx_chunk = min(512, cap)
    blk = pl.BlockSpec((1, r, LANES), lambda i: (i, 0, 0))
    return pl.pallas_call(
        functools.partial(_route_kernel, cap=cap, idx_chunk=idx_chunk),
        grid=(e,),
        in_specs=[blk],
        out_specs=[blk, pl.BlockSpec((1, cap, 1), lambda i: (i, 0, 0)), blk],
        out_shape=[jax.ShapeDtypeStruct((e, r, LANES), I32), jax.ShapeDtypeStruct((e, cap, 1), I32),
                   jax.ShapeDtypeStruct((e, r, LANES), I32)],
        compiler_params=_cparams("parallel"),
    )(aff3)


def _ffn_kernel(idx_ref, h_hbm, wg_ref, wu_ref, wd_ref, o_ref, xbuf, sem, *, tm):
    def row_copy(r):
        return pltpu.make_async_copy(h_hbm.at[pl.ds(idx_ref[0, 0, r], 1), :], xbuf.at[pl.ds(r, 1), :], sem)

    def issue(r, c):
        row_copy(r).start()
        return c

    def drain(r, c):
        row_copy(r).wait()
        return c

    lax.fori_loop(0, tm, issue, 0)
    lax.fori_loop(0, tm, drain, 0)
    xe = xbuf[...].astype(BF16)
    gate = jnp.dot(xe, wg_ref[0], preferred_element_type=F32)
    up = jnp.dot(xe, wu_ref[0], preferred_element_type=F32)
    hid = (gate * jax.nn.sigmoid(gate) * up).astype(BF16)
    o_ref[0] = jnp.dot(hid, wd_ref[0], preferred_element_type=F32).astype(o_ref.dtype)


def _expert_ffn(idx, h, w_gate, w_up, w_down, tm=256):
    e, cap = idx.shape
    d = h.shape[1]
    ff = w_gate.shape[2]
    nb = cap // tm
    return pl.pallas_call(
        functools.partial(_ffn_kernel, tm=tm),
        grid=(e, nb),
        in_specs=[pl.BlockSpec((1, 1, tm), lambda i, j: (i * nb + j, 0, 0), memory_space=pltpu.SMEM),
                  pl.BlockSpec(memory_space=pl.ANY),
                  pl.BlockSpec((1, d, ff), lambda i, j: (i, 0, 0)),
                  pl.BlockSpec((1, d, ff), lambda i, j: (i, 0, 0)),
                  pl.BlockSpec((1, ff, d), lambda i, j: (i, 0, 0))],
        out_specs=pl.BlockSpec((1, tm, d), lambda i, j: (i, j, 0)),
        out_shape=jax.ShapeDtypeStruct((e, cap, d), BF16),
        scratch_shapes=[pltpu.VMEM((tm, d), F32), pltpu.SemaphoreType.DMA(())],
        compiler_params=_cparams("arbitrary", "arbitrary"),
    )(idx.reshape(e * nb, 1, tm), h, w_gate, w_up, w_down)


def _combine_kernel(starts_ref, pos_ref, aff_ref, x_ref, ye_hbm, o_ref, win, win2, acc_ref, sem, sem2,
                    *, tm, cap, nt):
    i = pl.program_id(0)
    n_exp = pos_ref.shape[1]
    align = 16

    def base_of(e):
        start = starts_ref[e * (nt + 1) + i]
        return pl.multiple_of(jnp.minimum((start // align) * align, cap - tm), align)

    def win_copy(e, slot):
        return pltpu.make_async_copy(ye_hbm.at[e, pl.ds(base_of(e), tm), :], win.at[slot], sem.at[slot])

    lane = lax.broadcasted_iota(I32, (tm, tm), 1)
    acc_ref[...] = x_ref[...]
    win_copy(0, 0).start()
    for e in range(n_exp):
        slot = e % 2
        if e + 1 < n_exp:
            win_copy(e + 1, 1 - slot).start()
        base = base_of(e)
        end = starts_ref[e * (nt + 1) + i + 1]
        pos = pos_ref[:, e:e + 1]
        gate = aff_ref[:, e:e + 1]
        win_copy(e, slot).wait()
        onehot = (pos - base == lane).astype(BF16)
        acc_ref[...] += gate * jnp.dot(onehot, win[slot], preferred_element_type=F32)

        @pl.when(end > base + tm)
        def _():
            base2 = pl.multiple_of(jnp.minimum(base + tm, cap - tm), align)
            cp = pltpu.make_async_copy(ye_hbm.at[e, pl.ds(base2, tm), :], win2, sem2)
            cp.start()
            cp.wait()
            onehot2 = ((pos - base2 == lane) & (pos >= base + tm)).astype(BF16)
            acc_ref[...] += gate * jnp.dot(onehot2, win2[...], preferred_element_type=F32)

    o_ref[...] = acc_ref[...]


def _combine(starts, pos_t, aff_t, x, ye, tm=256):
    n, d = x.shape
    e, cap, _ = ye.shape
    nt = n // tm
    grid_spec = pltpu.PrefetchScalarGridSpec(
        num_scalar_prefetch=1,
        grid=(nt,),
        in_specs=[pl.BlockSpec((tm, e), lambda i, s: (i, 0)),
                  pl.BlockSpec((tm, e), lambda i, s: (i, 0)),
                  pl.BlockSpec((tm, d), lambda i, s: (i, 0)),
                  pl.BlockSpec(memory_space=pl.ANY)],
        out_specs=pl.BlockSpec((tm, d), lambda i, s: (i, 0)),
        scratch_shapes=[pltpu.VMEM((2, tm, d), BF16), pltpu.VMEM((tm, d), BF16), pltpu.VMEM((tm, d), F32),
                        pltpu.SemaphoreType.DMA((2,)), pltpu.SemaphoreType.DMA(())],
    )
    return pl.pallas_call(
        functools.partial(_combine_kernel, tm=tm, cap=cap, nt=nt),
        grid_spec=grid_spec,
        out_shape=jax.ShapeDtypeStruct((n, d), F32),
        compiler_params=_cparams("arbitrary"),
    )(starts.reshape(-1), pos_t, aff_t, x, ye)


def _expert_choice_ffn(x, g_ffn, w_router_t, w_gate, w_up, w_down, tm=256):
    n, _ = x.shape
    e = w_router_t.shape[0]
    cap = EC_CAPACITY * n // e
    h, aff = _norm_router(x, g_ffn, w_router_t)
    pos, idx, off = _route(aff.reshape(e, n // LANES, LANES), cap)
    rows_per_tile = tm // LANES
    starts = jnp.concatenate([off[:, ::rows_per_tile, 0], jnp.full((e, 1), cap, I32)], axis=1)
    ye = _expert_ffn(idx.reshape(e, cap), h, w_gate, w_up, w_down, tm=tm)
    return _combine(starts, pos.reshape(e, n).T, aff.T, x, ye, tm=tm)


def _prepare_layer(l, t, g_mix, w_in, pool_w, pool_scale, mla_gq, mla_gkv, mla_w_uq, mla_w_ukv, mla_q_norm,
                   mla_k_norm, na_q_norm, na_k_norm, na_rel_bias, b_gate, w_br_pool, w_br_mla, w_br_na, w_out,
                   g_ffn, w_router, w_e_gate, w_e_up, w_e_down):
    d = w_in.shape[1]
    off_cq = POOL_WIDTH
    off_na = off_cq + 2 * MLA_RANK + MLA_ROPE
    off_gate = off_na + 3 * NA_WIDTH
    wl = w_in[l]
    p = {}
    p["g_mix"] = g_mix[l]
    p["w_pool_in"] = wl[:, :off_cq].astype(BF16)
    p["w_mla_in"] = jnp.pad(wl[:, off_cq:off_na], ((0, 0), (0, MLA_SEG - (off_na - off_cq)))).astype(BF16)
    p["w_na_in"] = wl[:, off_na:off_gate].astype(BF16)
    p["w_gate_in"] = wl[:, off_gate:].astype(BF16)
    p["b_gate"] = b_gate[l].reshape(N_BRANCH * d)
    p["pool_w"] = pool_w[l].astype(BF16)
    p["pool_scale"] = pool_scale[l]
    p["gq"] = mla_gq[l].reshape(1, MLA_RANK)
    p["gkv"] = mla_gkv[l].reshape(1, MLA_RANK)
    wq = mla_w_uq[l].reshape(MLA_RANK, MLA_HEADS, MLA_QK)
    wq_rope = jnp.pad(wq[:, :, MLA_NOPE:], ((0, 0), (0, 0), (0, LANES - MLA_ROPE)))
    p["wuq"] = jnp.concatenate([wq[:, :, :MLA_NOPE].reshape(MLA_RANK, -1), wq_rope.reshape(MLA_RANK, -1)],
                               axis=1).astype(BF16)
    wkv = mla_w_ukv[l].reshape(MLA_RANK, MLA_HEADS, MLA_NOPE + MLA_V)
    p["wukv"] = jnp.concatenate([wkv[:, :, :MLA_NOPE].reshape(MLA_RANK, -1), wkv[:, :, MLA_NOPE:].reshape(MLA_RANK, -1)],
                                axis=1).astype(BF16)
    pad_norm = lambda g: jnp.pad(g, (0, 2 * LANES - MLA_QK)).reshape(1, 2 * LANES)
    p["qn"] = pad_norm(mla_q_norm[l] * (MLA_QK ** -0.5))
    p["kn"] = pad_norm(mla_k_norm[l])
    p["na_qn"] = jnp.tile(na_q_norm[l] * (NA_HEAD_DIM ** -0.5), 2).reshape(1, LANES)
    p["na_kn"] = jnp.tile(na_k_norm[l], 2).reshape(1, LANES)
    p["na_bias"] = _na_bias_tables(na_rel_bias[l], t // GRID_W)
    p["w_br_pool"] = w_br_pool[l].astype(BF16)
    p["w_br_mla"] = w_br_mla[l].astype(BF16)
    p["w_br_na"] = w_br_na[l].astype(BF16)
    p["w_out"] = w_out[l].astype(BF16)
    p["g_ffn"] = g_ffn[l]
    p["w_router_t"] = w_router[l].T
    p["w_e_gate"] = w_e_gate[l].astype(BF16)
    p["w_e_up"] = w_e_up[l].astype(BF16)
    p["w_e_down"] = w_e_down[l].astype(BF16)
    return p


def _rope_tables(t):
    half = MLA_ROPE // 2
    pos = jnp.arange(t, dtype=F32)
    inv = 1.0 / (ROPE_THETA ** (jnp.arange(0, MLA_ROPE, 2, dtype=F32) / MLA_ROPE))
    ang = pos[:, None] * inv[None, :]
    cos = jnp.cos(ang)
    sin = jnp.sin(ang)
    zeros = jnp.zeros((t, LANES - MLA_ROPE), F32)
    return jnp.concatenate([cos, cos, zeros], axis=1), jnp.concatenate([-sin, sin, zeros], axis=1)


def _layer(x, p, cos_t, sin_t, b, t):
    m, d = x.shape
    h = _rmsnorm(x, p["g_mix"])
    u = _matmul(h, p["w_pool_in"], F32, tn=512)
    mla_seg = _matmul(h, p["w_mla_in"], F32, tn=MLA_SEG // 3)
    na_qkv = _matmul(h, p["w_na_in"], F32, tn=512)
    gates = _matmul(h, p["w_gate_in"], BF16, tn=512, bias=p["b_gate"])
    a_pool = _pool_mixer(u.reshape(b, t, POOL_WIDTH), p["pool_w"], p["pool_scale"]).reshape(m, POOL_WIDTH)
    q, k, v = _mla_prep(mla_seg, p["gq"], p["gkv"], p["wuq"], p["wukv"], p["qn"], p["kn"], cos_t, sin_t, b, t)
    a_mla = _mla_attn(q, k, v).reshape(m, MLA_HEADS * MLA_V)
    a_na = _na_attn(na_qkv.reshape(b, t, 3 * NA_WIDTH), p["na_qn"], p["na_kn"], p["na_bias"], b, t).reshape(m, NA_WIDTH)
    merged = _merge(a_pool, a_mla, a_na, p["w_br_pool"], p["w_br_mla"], p["w_br_na"], gates)
    x = _matmul(merged, p["w_out"], F32, tn=512, residual=x)
    return _expert_choice_ffn(x, p["g_ffn"], p["w_router_t"], p["w_e_gate"], p["w_e_up"], p["w_e_down"])


def _trunk(x, layers, cos_t, sin_t):
    b, t, d = x.shape
    x = x.reshape(b * t, d)
    for p in layers:
        x = _layer(x, p, cos_t, sin_t, b, t)
    return x.reshape(b, t, d)


def kernel(x_prompt, x_sample, g_mix, w_in, pool_w, pool_scale, mla_gq, mla_gkv, mla_w_uq, mla_w_ukv, mla_q_norm,
           mla_k_norm, na_q_norm, na_k_norm, na_rel_bias, b_gate, w_br_pool, w_br_mla, w_br_na, w_out, g_ffn,
           w_router, w_e_gate, w_e_up, w_e_down):
    weights = (g_mix, w_in, pool_w, pool_scale, mla_gq, mla_gkv, mla_w_uq, mla_w_ukv, mla_q_norm, mla_k_norm,
               na_q_norm, na_k_norm, na_rel_bias, b_gate, w_br_pool, w_br_mla, w_br_na, w_out, g_ffn, w_router,
               w_e_gate, w_e_up, w_e_down)
    prepared = {}
    outs = []
    for x in (x_prompt, x_sample):
        t = x.shape[1]
        if t not in prepared:
            prepared[t] = ([_prepare_layer(l, t, *weights) for l in range(g_mix.shape[0])], _rope_tables(t))
        layers, (cos_t, sin_t) = prepared[t]
        outs.append(_trunk(x, layers, cos_t, sin_t))
    return tuple(outs)
```

```python
import functools
import math

import numpy as np
import jax
import jax.numpy as jnp
from jax import lax
from jax.experimental import pallas as pl
from jax.experimental.pallas import tpu as pltpu

F32 = jnp.float32
BF16 = jnp.bfloat16
I32 = jnp.int32

LANES = 128
VMEM_LIMIT = 56 * 1024 * 1024

GRID_W = 64
POOL_GROUPS = 4
POOL_GROUP_DIM = 256
POOL_WIDTH = POOL_GROUPS * POOL_GROUP_DIM
POOL_WINDOWS = (2, 4, 8, 16)
POOL_HALO = 8
MLA_HEADS = 16
MLA_RANK = 512
MLA_NOPE = 128
MLA_ROPE = 64
MLA_QK = MLA_NOPE + MLA_ROPE
MLA_QK_PAD = 256
MLA_V = 128
MLA_SEG = 2 * MLA_RANK + LANES
ROPE_THETA = 10000.0
NA_HEADS = 16
NA_HEAD_DIM = 64
NA_WIDTH = NA_HEADS * NA_HEAD_DIM
NA_KH = 8
NA_KW = 16
NA_QROWS = 8
NA_KROWS = 16
N_BRANCH = 3
N_EXPERTS = 16
EC_CAPACITY = 2
EPS = 1e-6
NEG = -1e30
LOG2E = math.log2(math.e)


def _cparams(*sem):
    return pltpu.CompilerParams(dimension_semantics=sem, vmem_limit_bytes=VMEM_LIMIT)


def _rmsnorm_kernel(x_ref, g_ref, o_ref):
    x = x_ref[...]
    y = x * lax.rsqrt(jnp.mean(x * x, axis=-1, keepdims=True) + EPS) * g_ref[...]
    o_ref[...] = y.astype(o_ref.dtype)


def _rmsnorm(x, g, tm=512):
    m, d = x.shape
    return pl.pallas_call(
        _rmsnorm_kernel,
        grid=(m // tm,),
        in_specs=[pl.BlockSpec((tm, d), lambda i: (i, 0)), pl.BlockSpec((1, d), lambda i: (0, 0))],
        out_specs=pl.BlockSpec((tm, d), lambda i: (i, 0)),
        out_shape=jax.ShapeDtypeStruct((m, d), BF16),
        name="rmsnorm",
        compiler_params=_cparams("parallel"),
    )(x, g.reshape(1, d))


def _norm_router_kernel(x_ref, g_ref, wr_ref, h_ref, aff_ref):
    x = x_ref[...]
    y = x * lax.rsqrt(jnp.mean(x * x, axis=-1, keepdims=True) + EPS) * g_ref[...]
    h_ref[...] = y
    logits = lax.dot_general(wr_ref[...], y, (((1,), (1,)), ((), ())), precision=lax.Precision.HIGHEST,
                             preferred_element_type=F32)
    mx = jnp.max(logits, axis=0, keepdims=True)
    ex = jnp.exp(logits - mx)
    aff_ref[...] = ex / jnp.sum(ex, axis=0, keepdims=True)


def _norm_router(x, g, w_router_t, tm=512):
    m, d = x.shape
    e = w_router_t.shape[0]
    return pl.pallas_call(
        _norm_router_kernel,
        grid=(m // tm,),
        in_specs=[pl.BlockSpec((tm, d), lambda i: (i, 0)), pl.BlockSpec((1, d), lambda i: (0, 0)),
                  pl.BlockSpec((e, d), lambda i: (0, 0))],
        out_specs=[pl.BlockSpec((tm, d), lambda i: (i, 0)), pl.BlockSpec((e, tm), lambda i: (0, i))],
        out_shape=[jax.ShapeDtypeStruct((m, d), F32), jax.ShapeDtypeStruct((e, m), F32)],
        name="norm_router",
        compiler_params=_cparams("parallel"),
    )(x, g.reshape(1, d), w_router_t)


def _mm_kernel(a_ref, w_ref, *rest, epilogue):
    acc = jnp.dot(a_ref[...], w_ref[...], preferred_element_type=F32)
    if epilogue == "sigmoid_bias":
        b_ref, o_ref = rest
        acc = jax.nn.sigmoid(acc + b_ref[...])
    elif epilogue == "residual":
        r_ref, o_ref = rest
        acc = acc + r_ref[...]
    else:
        (o_ref,) = rest
    o_ref[...] = acc.astype(o_ref.dtype)


def _matmul(a, w, out_dtype, tn, name, tm=1024, bias=None, residual=None):
    m, k = a.shape
    n = w.shape[1]
    in_specs = [pl.BlockSpec((tm, k), lambda j, i: (i, 0)), pl.BlockSpec((k, tn), lambda j, i: (0, j))]
    args = [a, w]
    epilogue = "none"
    if bias is not None:
        epilogue = "sigmoid_bias"
        in_specs.append(pl.BlockSpec((1, tn), lambda j, i: (0, j)))
        args.append(bias.reshape(1, n))
    if residual is not None:
        epilogue = "residual"
        in_specs.append(pl.BlockSpec((tm, tn), lambda j, i: (i, j)))
        args.append(residual)
    return pl.pallas_call(
        functools.partial(_mm_kernel, epilogue=epilogue),
        grid=(n // tn, m // tm),
        in_specs=in_specs,
        out_specs=pl.BlockSpec((tm, tn), lambda j, i: (i, j)),
        out_shape=jax.ShapeDtypeStruct((m, n), out_dtype),
        name=name,
        compiler_params=_cparams("parallel", "parallel"),
    )(*args)


def _pool_kernel(u_ref, w_ref, s_ref, o_ref, pad_ref, *, seq, chunk):
    g = pl.program_id(1)
    zeros = jnp.zeros((POOL_HALO, POOL_GROUP_DIM), F32)
    pad_ref[0:POOL_HALO, :] = zeros
    pad_ref[POOL_HALO + seq:2 * POOL_HALO + seq, :] = zeros
    pad_ref[POOL_HALO:POOL_HALO + seq, :] = u_ref[0]

    def window(w):
        left = w // 2
        right = w - 1 - left
        for c in range(seq // chunk):
            base = POOL_HALO + c * chunk
            acc = pad_ref[base - left:base - left + chunk, :]
            for d in range(-left + 1, right + 1):
                acc = acc + pad_ref[base + d:base + d + chunk, :]
            t = c * chunk + lax.broadcasted_iota(I32, (chunk, 1), 0)
            cnt = (jnp.minimum(t + right, seq - 1) + 1 - jnp.maximum(t - left, 0)).astype(F32)
            pooled = acc / cnt - pad_ref[base:base + chunk, :]
            mixed = jnp.dot(pooled.astype(BF16), w_ref[0], preferred_element_type=F32) * s_ref[...]
            o_ref[0, c * chunk:(c + 1) * chunk, :] = mixed.astype(o_ref.dtype)

    for k, w in enumerate(POOL_WINDOWS):
        pl.when(g == k)(functools.partial(window, w))


def _pool_mixer(u, pool_w, pool_scale):
    b, t, _ = u.shape
    chunk = min(512, t)
    return pl.pallas_call(
        functools.partial(_pool_kernel, seq=t, chunk=chunk),
        grid=(b, POOL_GROUPS),
        in_specs=[pl.BlockSpec((1, t, POOL_GROUP_DIM), lambda i, g: (i, 0, g)),
                  pl.BlockSpec((1, POOL_GROUP_DIM, POOL_GROUP_DIM), lambda i, g: (g, 0, 0)),
                  pl.BlockSpec((1, POOL_GROUP_DIM), lambda i, g: (0, g))],
        out_specs=pl.BlockSpec((1, t, POOL_GROUP_DIM), lambda i, g: (i, 0, g)),
        out_shape=jax.ShapeDtypeStruct((b, t, POOL_WIDTH), BF16),
        scratch_shapes=[pltpu.VMEM((t + 2 * POOL_HALO, POOL_GROUP_DIM), F32)],
        name="pool_mixer",
        compiler_params=_cparams("parallel", "parallel"),
    )(u, pool_w, pool_scale.reshape(1, POOL_WIDTH))


def _mla_prep_kernel(seg_ref, gq_ref, gkv_ref, wuq_ref, wukv_ref, qn_ref, kn_ref, cos_ref, sin_ref,
                     q_ref, k_ref, v_ref):
    def rms(x):
        return x * lax.rsqrt(jnp.mean(x * x, axis=-1, keepdims=True) + EPS)

    def rope(x):
        lane = lax.broadcasted_iota(I32, x.shape, 1)
        partner = jnp.where(lane < MLA_ROPE // 2, pltpu.roll(x, LANES - MLA_ROPE // 2, 1),
                            pltpu.roll(x, MLA_ROPE // 2, 1))
        return x * cos_ref[...] + partner * sin_ref[...]

    cq = (rms(seg_ref[:, 0:MLA_RANK]) * gq_ref[...]).astype(BF16)
    ckv = (rms(seg_ref[:, MLA_RANK:2 * MLA_RANK]) * gkv_ref[...]).astype(BF16)
    kr = seg_ref[:, 2 * MLA_RANK:2 * MLA_RANK + LANES]
    q = jnp.dot(cq, wuq_ref[...], preferred_element_type=F32)
    kv = jnp.dot(ckv, wukv_ref[...], preferred_element_type=F32)
    hn = MLA_HEADS * MLA_NOPE
    kr_sq = kr * kr
    kr_rot = rope(kr * kn_ref[:, LANES:2 * LANES])
    inv_d = 1.0 / MLA_QK
    for h in range(MLA_HEADS):
        qa = q[:, h * LANES:(h + 1) * LANES]
        qb = q[:, hn + h * LANES:hn + (h + 1) * LANES]
        rs = lax.rsqrt(jnp.sum(qa * qa + qb * qb, axis=-1, keepdims=True) * inv_d + EPS)
        q_ref[0, h, :, 0:LANES] = (qa * rs * qn_ref[:, 0:LANES]).astype(BF16)
        q_ref[0, h, :, LANES:2 * LANES] = rope(qb * rs * qn_ref[:, LANES:2 * LANES]).astype(BF16)
        ka = kv[:, h * LANES:(h + 1) * LANES]
        rs = lax.rsqrt(jnp.sum(ka * ka + kr_sq, axis=-1, keepdims=True) * inv_d + EPS)
        k_ref[0, h, :, 0:LANES] = (ka * rs * kn_ref[:, 0:LANES]).astype(BF16)
        k_ref[0, h, :, LANES:2 * LANES] = (kr_rot * rs).astype(BF16)
        v_ref[0, h, :, :] = kv[:, hn + h * LANES:hn + (h + 1) * LANES].astype(BF16)


def _mla_prep(seg, gq, gkv, wuq, wukv, qn, kn, cos_t, sin_t, b, t, tm=256):
    m = b * t
    tpb = t // tm
    const = lambda i: (0, 0)
    head_map = lambda i: (i // tpb, 0, i % tpb, 0)
    return pl.pallas_call(
        _mla_prep_kernel,
        grid=(m // tm,),
        in_specs=[pl.BlockSpec((tm, MLA_SEG), lambda i: (i, 0)),
                  pl.BlockSpec((1, MLA_RANK), const), pl.BlockSpec((1, MLA_RANK), const),
                  pl.BlockSpec(wuq.shape, const), pl.BlockSpec(wukv.shape, const),
                  pl.BlockSpec((1, 2 * LANES), const), pl.BlockSpec((1, 2 * LANES), const),
                  pl.BlockSpec((tm, LANES), lambda i: (i % tpb, 0)), pl.BlockSpec((tm, LANES), lambda i: (i % tpb, 0))],
        out_specs=[pl.BlockSpec((1, MLA_HEADS, tm, MLA_QK_PAD), head_map),
                   pl.BlockSpec((1, MLA_HEADS, tm, MLA_QK_PAD), head_map),
                   pl.BlockSpec((1, MLA_HEADS, tm, MLA_V), head_map)],
        out_shape=[jax.ShapeDtypeStruct((b, MLA_HEADS, t, MLA_QK_PAD), BF16),
                   jax.ShapeDtypeStruct((b, MLA_HEADS, t, MLA_QK_PAD), BF16),
                   jax.ShapeDtypeStruct((b, MLA_HEADS, t, MLA_V), BF16)],
        name="mla_prep",
        compiler_params=_cparams("parallel"),
    )(seg, gq, gkv, wuq, wukv, qn, kn, cos_t, sin_t)


def _mla_attn_kernel(q_ref, k_ref, v_ref, o_ref, *, sub):
    for r in range(q_ref.shape[2] // sub):
        rows = slice(r * sub, (r + 1) * sub)
        q = q_ref[0, 0, rows, :]
        s = lax.dot_general(q, k_ref[0, 0], (((1,), (1,)), ((), ())), preferred_element_type=F32)
        m = jnp.max(s, axis=-1, keepdims=True)
        p = jnp.exp2(s - m)
        l = jnp.sum(p, axis=-1, keepdims=True)
        acc = jnp.dot(p.astype(BF16), v_ref[0, 0], preferred_element_type=F32)
        o_ref[0, rows, :] = (acc / l).astype(o_ref.dtype)


def _mla_attn(q, k, v, tq=1024, sub=256):
    b, h, t, _ = q.shape
    return pl.pallas_call(
        functools.partial(_mla_attn_kernel, sub=sub),
        grid=(b, h, t // tq),
        in_specs=[pl.BlockSpec((1, 1, tq, MLA_QK_PAD), lambda i, j, n: (i, j, n, 0)),
                  pl.BlockSpec((1, 1, t, MLA_QK_PAD), lambda i, j, n: (i, j, 0, 0)),
                  pl.BlockSpec((1, 1, t, MLA_V), lambda i, j, n: (i, j, 0, 0))],
        out_specs=pl.BlockSpec((1, tq, MLA_V), lambda i, j, n: (i, n, j)),
        out_shape=jax.ShapeDtypeStruct((b, t, h * MLA_V), BF16),
        name="mla_attn",
        compiler_params=_cparams("parallel", "parallel", "parallel"),
    )(q, k, v)


def _na_bias_tables(rel_bias, rows):
    kw = NA_KW
    exact = lax.Precision.HIGHEST
    qc = np.arange(GRID_W)[:, None]
    kc = np.arange(GRID_W)[None, :]
    sc = np.clip(qc - kw // 2, 0, GRID_W - kw)
    valid_c = (kc >= sc) & (kc < sc + kw)
    dx = np.clip(kc - qc + kw - 1, 0, 2 * kw - 2)
    onehot_x = (dx[:, :, None] == np.arange(2 * kw - 1)) & valid_c[:, :, None]
    by_col = jnp.einsum("hyx,qkx->hyqk", rel_bias, jnp.asarray(onehot_x, F32), precision=exact)
    qr = np.arange(NA_QROWS)[:, None]
    kr = np.arange(NA_KROWS)[None, :]
    tables = []
    for r0 in (0, NA_QROWS, rows - NA_QROWS):
        ks = int(np.clip(r0 - NA_KH // 2, 0, rows - NA_KROWS))
        qrow = r0 + qr
        krow = ks + kr
        sr = np.clip(qrow - NA_KH // 2, 0, rows - NA_KH)
        valid_r = (krow >= sr) & (krow < sr + NA_KH)
        dy = np.clip(krow - qrow + NA_KH - 1, 0, 2 * NA_KH - 2)
        onehot_y = (dy[:, :, None] == np.arange(2 * NA_KH - 1)) & valid_r[:, :, None]
        tbl = jnp.einsum("hyqk,rsy->hrqsk", by_col, jnp.asarray(onehot_y, F32), precision=exact)
        valid = valid_r[:, None, :, None] & valid_c[None, :, None, :]
        tbl = jnp.where(valid[None], tbl * LOG2E, NEG)
        tables.append(tbl.reshape(-1, NA_QROWS * GRID_W, NA_KROWS * GRID_W))
    return jnp.stack(tables, axis=1)


def _na_kernel(q_ref, k_ref, v_ref, qn_ref, kn_ref, bias_ref, o_ref, kn_s, v_s, *, rows, chunk):
    rb = pl.program_id(2)
    nrb = pl.num_programs(2)
    seq = rows * GRID_W
    inv_d = 1.0 / NA_HEAD_DIM

    def head_norm(x, g):
        lane = lax.broadcasted_iota(I32, x.shape, 1)
        lo = lane < NA_HEAD_DIM
        sq = x * x
        s0 = jnp.sum(jnp.where(lo, sq, 0.0), axis=-1, keepdims=True)
        s1 = jnp.sum(jnp.where(lo, 0.0, sq), axis=-1, keepdims=True)
        rs = lax.rsqrt(jnp.where(lo, s0, s1) * inv_d + EPS)
        return x * rs * g

    @pl.when(rb == 0)
    def _():
        for c in range(seq // chunk):
            sl = slice(c * chunk, (c + 1) * chunk)
            kn_s[sl, :] = head_norm(k_ref[0, sl, :], kn_ref[...]).astype(BF16)
            v_s[sl, :] = v_ref[0, sl, :].astype(BF16)

    qn = head_norm(q_ref[0], qn_ref[...])
    ks = jnp.clip(rb * NA_QROWS - NA_KH // 2, 0, rows - NA_KROWS)
    start = pl.multiple_of(ks * GRID_W, GRID_W)
    kwin = kn_s[pl.ds(start, NA_KROWS * GRID_W), :]
    vwin = v_s[pl.ds(start, NA_KROWS * GRID_W), :]
    ty = jnp.where(rb == 0, 0, jnp.where(rb == nrb - 1, 2, 1))
    lane = lax.broadcasted_iota(I32, qn.shape, 1)
    outs = []
    for a in range(2):
        mine = lane < NA_HEAD_DIM if a == 0 else lane >= NA_HEAD_DIM
        qa = jnp.where(mine, qn, 0.0).astype(BF16)
        s = lax.dot_general(qa, kwin, (((1,), (1,)), ((), ())), preferred_element_type=F32)
        s = s + bias_ref[a, ty]
        m = jnp.max(s, axis=-1, keepdims=True)
        p = jnp.exp2(s - m)
        l = jnp.sum(p, axis=-1, keepdims=True)
        outs.append(jnp.dot(p.astype(BF16), vwin, preferred_element_type=F32) / l)
    o_ref[0] = jnp.where(lane < NA_HEAD_DIM, outs[0], outs[1]).astype(o_ref.dtype)


def _na_attn(qkv, qn, kn, bias_tables, b, t):
    rows = t // GRID_W
    nrb = rows // NA_QROWS
    nhp = NA_WIDTH // LANES
    tq = NA_QROWS * GRID_W
    tkw = NA_KROWS * GRID_W
    return pl.pallas_call(
        functools.partial(_na_kernel, rows=rows, chunk=min(512, t)),
        grid=(nhp, b, nrb),
        in_specs=[pl.BlockSpec((1, tq, LANES), lambda hp, i, r: (i, r, hp)),
                  pl.BlockSpec((1, t, LANES), lambda hp, i, r: (i, 0, nhp + hp)),
                  pl.BlockSpec((1, t, LANES), lambda hp, i, r: (i, 0, 2 * nhp + hp)),
                  pl.BlockSpec((1, LANES), lambda hp, i, r: (0, 0)),
                  pl.BlockSpec((1, LANES), lambda hp, i, r: (0, 0)),
                  pl.BlockSpec((2, 3, tq, tkw), lambda hp, i, r: (hp, 0, 0, 0))],
        out_specs=pl.BlockSpec((1, tq, LANES), lambda hp, i, r: (i, r, hp)),
        out_shape=jax.ShapeDtypeStruct((b, t, NA_WIDTH), BF16),
        scratch_shapes=[pltpu.VMEM((t, LANES), BF16), pltpu.VMEM((t, LANES), BF16)],
        name="na_attn",
        compiler_params=_cparams("parallel", "parallel", "arbitrary"),
    )(qkv, qkv, qkv, qn, kn, bias_tables)


def _merge_kernel(ap_ref, am_ref, an_ref, wp_ref, wm_ref, wn_ref, g0_ref, g1_ref, g2_ref, o_ref):
    acc = g0_ref[...].astype(F32) * jnp.dot(ap_ref[...], wp_ref[...], preferred_element_type=F32)
    acc = acc + g1_ref[...].astype(F32) * jnp.dot(am_ref[...], wm_ref[...], preferred_element_type=F32)
    acc = acc + g2_ref[...].astype(F32) * jnp.dot(an_ref[...], wn_ref[...], preferred_element_type=F32)
    o_ref[...] = acc.astype(o_ref.dtype)


def _merge(a_pool, a_mla, a_na, w_pool, w_mla, w_na, gates, tm=1024, tn=512):
    m = a_pool.shape[0]
    d = w_pool.shape[1]
    nj = d // tn
    a_spec = lambda a: pl.BlockSpec((tm, a.shape[1]), lambda j, i: (i, 0))
    w_spec = lambda w: pl.BlockSpec((w.shape[0], tn), lambda j, i: (0, j))
    g_spec = lambda br: pl.BlockSpec((tm, tn), lambda j, i: (i, br * nj + j))
    return pl.pallas_call(
        _merge_kernel,
        grid=(nj, m // tm),
        in_specs=[a_spec(a_pool), a_spec(a_mla), a_spec(a_na), w_spec(w_pool), w_spec(w_mla), w_spec(w_na),
                  g_spec(0), g_spec(1), g_spec(2)],
        out_specs=pl.BlockSpec((tm, tn), lambda j, i: (i, j)),
        out_shape=jax.ShapeDtypeStruct((m, d), BF16),
        name="branch_merge",
        compiler_params=_cparams("parallel", "parallel"),
    )(a_pool, a_mla, a_na, w_pool, w_mla, w_na, gates, gates, gates)


def _route_kernel(aff_ref, pos_ref, idx_ref, gate_ref, off_ref, *, cap, idx_chunk):
    a = aff_ref[0]
    r = a.shape[0]
    n = r * LANES
    bits = lax.bitcast_convert_type(a, I32)
    tok = lax.broadcasted_iota(I32, (r, LANES), 0) * LANES + lax.broadcasted_iota(I32, (r, LANES), 1)

    def count(mask):
        return jnp.sum(jnp.sum(mask.astype(F32), axis=1, keepdims=True), axis=0, keepdims=True)

    thr = jnp.zeros((1, 1), I32)
    for bit in range(30, -1, -1):
        cand = thr | (1 << bit)
        thr = jnp.where(count(bits >= cand) >= cap, cand, thr)
    above = bits > thr
    tied = bits == thr
    need = cap - count(above)
    last = jnp.zeros((1, 1), I32)
    for bit in range(int(math.log2(n)) - 1, -1, -1):
        cand = last | (1 << bit)
        last = jnp.where(count(tied & (tok < cand)) < need, cand, last)
    sel = above | (tied & (tok <= last))
    m = sel.astype(BF16)

    li = lax.broadcasted_iota(I32, (LANES, LANES), 0)
    lj = lax.broadcasted_iota(I32, (LANES, LANES), 1)
    upper = (li <= lj).astype(BF16)
    c1 = jnp.dot(m, upper, preferred_element_type=F32)
    rowtot = jnp.broadcast_to(c1[:, LANES - 1:LANES], (r, LANES))
    ri = lax.broadcasted_iota(I32, (r, r), 0)
    rj = lax.broadcasted_iota(I32, (r, r), 1)
    rowoff = jnp.dot((rj < ri).astype(BF16), rowtot.astype(BF16), preferred_element_type=F32)
    pos_ref[0] = jnp.where(sel, c1 - 1.0 + rowoff, -1.0).astype(I32)
    off_ref[0] = rowoff.astype(I32)

    tot_l = lax.dot_general(jnp.ones((8, LANES), BF16), m, (((1,), (1,)), ((), ())),
                            preferred_element_type=F32)
    cum_l = jnp.dot(tot_l.astype(BF16), (ri <= rj).astype(BF16), preferred_element_type=F32)[0:1, :]
    excl_l = cum_l - tot_l[0:1, :]
    lane_r = lax.broadcasted_iota(I32, (idx_chunk, r), 1)
    lane_t = lax.broadcasted_iota(I32, (idx_chunk, LANES), 1)
    for c in range(cap // idx_chunk):
        p = (c * idx_chunk + lax.broadcasted_iota(I32, (idx_chunk, 1), 0)).astype(F32)
        row = jnp.sum((cum_l <= p).astype(F32), axis=1, keepdims=True)
        onehot = lane_r == row.astype(I32)
        off = jnp.sum(jnp.where(onehot, excl_l, 0.0), axis=1, keepdims=True)
        mrow = jnp.dot(onehot.astype(BF16), m, preferred_element_type=F32)
        crow = jnp.dot(mrow.astype(BF16), upper, preferred_element_type=F32)
        lane = jnp.sum((crow <= p - off).astype(F32), axis=1, keepdims=True)
        arow = jnp.dot(onehot.astype(F32), a, precision=lax.Precision.HIGHEST, preferred_element_type=F32)
        sl = slice(c * idx_chunk, (c + 1) * idx_chunk)
        idx_ref[0, sl, :] = (row * LANES + lane).astype(I32)
        gate_ref[0, sl, :] = jnp.sum(jnp.where(lane_t == lane.astype(I32), arow, 0.0), axis=1, keepdims=True)


def _route(aff3, cap):
    e, r, _ = aff3.shape
    idx_chunk = min(512, cap)
    blk = pl.BlockSpec((1, r, LANES), lambda i: (i, 0, 0))
    slot_blk = pl.BlockSpec((1, cap, 1), lambda i: (i, 0, 0))
    return pl.pallas_call(
        functools.partial(_route_kernel, cap=cap, idx_chunk=idx_chunk),
        grid=(e,),
        in_specs=[blk],
        out_specs=[blk, slot_blk, slot_blk, blk],
        out_shape=[jax.ShapeDtypeStruct((e, r, LANES), I32), jax.ShapeDtypeStruct((e, cap, 1), I32),
                   jax.ShapeDtypeStruct((e, cap, 1), F32), jax.ShapeDtypeStruct((e, r, LANES), I32)],
        name="route",
        compiler_params=_cparams("parallel"),
    )(aff3)


def _ffn_kernel(idx_ref, h_hbm, gate_ref, wg_ref, wu_ref, wd_ref, o_ref, xbuf, sem, *, tm):
    nb = pl.num_programs(1)
    step = pl.program_id(0) * nb + pl.program_id(1)
    nsteps = pl.num_programs(0) * nb
    slot = step % 2

    def row_copy(block, slot_, r):
        tok = idx_ref[block * tm + r]
        return pltpu.make_async_copy(h_hbm.at[pl.ds(tok, 1), :], xbuf.at[slot_, pl.ds(r, 1), :], sem.at[slot_])

    def wait_block(slot_):
        pltpu.make_async_copy(h_hbm.at[pl.ds(0, tm), :], xbuf.at[slot_], sem.at[slot_]).wait()

    @pl.when(step == 0)
    def _():
        def issue(r, c):
            row_copy(0, 0, r).start()
            return c
        lax.fori_loop(0, tm, issue, 0, unroll=8)

    wait_block(slot)
    xe = xbuf[slot].astype(BF16)
    nxt = jnp.minimum(step + 1, nsteps - 1)
    for r in range(tm):
        row_copy(nxt, 1 - slot, r).start()
    gate = jnp.dot(xe, wg_ref[0], preferred_element_type=F32)
    up = jnp.dot(xe, wu_ref[0], preferred_element_type=F32)
    hid = (gate * jax.nn.sigmoid(gate) * up).astype(BF16)
    ye = jnp.dot(hid, wd_ref[0], preferred_element_type=F32)
    o_ref[0] = (ye * gate_ref[0]).astype(o_ref.dtype)

    @pl.when(step == nsteps - 1)
    def _():
        wait_block(1 - slot)


def _expert_ffn(idx, slot_gate, h, w_gate, w_up, w_down, tm=256):
    e, cap, _ = slot_gate.shape
    d = h.shape[1]
    ff = w_gate.shape[2]
    nb = cap // tm
    grid_spec = pltpu.PrefetchScalarGridSpec(
        num_scalar_prefetch=1,
        grid=(e, nb),
        in_specs=[pl.BlockSpec(memory_space=pl.ANY),
                  pl.BlockSpec((1, tm, 1), lambda i, j, s: (i, j, 0)),
                  pl.BlockSpec((1, d, ff), lambda i, j, s: (i, 0, 0)),
                  pl.BlockSpec((1, d, ff), lambda i, j, s: (i, 0, 0)),
                  pl.BlockSpec((1, ff, d), lambda i, j, s: (i, 0, 0))],
        out_specs=pl.BlockSpec((1, tm, d), lambda i, j, s: (i, j, 0)),
        scratch_shapes=[pltpu.VMEM((2, tm, d), F32), pltpu.SemaphoreType.DMA((2,))],
    )
    return pl.pallas_call(
        functools.partial(_ffn_kernel, tm=tm),
        grid_spec=grid_spec,
        out_shape=jax.ShapeDtypeStruct((e, cap, d), BF16),
        name="expert_ffn",
        compiler_params=_cparams("arbitrary", "arbitrary"),
    )(idx, h, slot_gate, w_gate, w_up, w_down)


COMBINE_WINDOW = 128
SLOT_ALIGN = 16


def _combine_kernel(starts_ref, pos_ref, x_ref, ye_hbm, o_ref, ycat, yextra, sem, sem_extra, *, tm, cap, nt):
    i = pl.program_id(0)
    n_exp = pos_ref.shape[1]
    w = COMBINE_WINDOW
    slot = i % 2

    def base_of(e, tile):
        start = starts_ref[e * (nt + 1) + tile]
        return jnp.minimum((start // SLOT_ALIGN) * SLOT_ALIGN, cap - w)

    def window_copy(e, first_row, dst, dst_sem):
        src = ye_hbm.at[e, pl.ds(pl.multiple_of(first_row, SLOT_ALIGN), w), :]
        return pltpu.make_async_copy(src, dst.at[pl.ds(e * w, w), :], dst_sem)

    def prefetch(tile, slot_):
        for e in range(n_exp):
            window_copy(e, base_of(e, tile), ycat.at[slot_], sem.at[slot_]).start()

    @pl.when(i == 0)
    def _():
        prefetch(0, 0)

    @pl.when(i + 1 < nt)
    def _():
        prefetch(i + 1, 1 - slot)

    bases = [base_of(e, i) for e in range(n_exp)]
    lane = lax.broadcasted_iota(I32, (tm, w), 1)

    def onehot(first_rows, floors):
        cols = []
        for e in range(n_exp):
            pos = pos_ref[:, e:e + 1]
            hit = pos - first_rows[e] == lane
            if floors is not None:
                hit = hit & (pos >= floors[e])
            cols.append(hit.astype(BF16))
        return jnp.concatenate(cols, axis=1)

    sel = onehot(bases, None)
    for e in range(n_exp):
        window_copy(e, bases[e], ycat.at[slot], sem.at[slot]).wait()
    o_ref[...] = x_ref[...] + jnp.dot(sel, ycat[slot], preferred_element_type=F32)

    npass = jnp.int32(0)
    for e in range(n_exp):
        end = starts_ref[e * (nt + 1) + i + 1]
        npass = jnp.maximum(npass, (end - bases[e] + w - 1) // w)

    def extra_pass(j, c):
        floors = [bases[e] + j * w for e in range(n_exp)]
        first_rows = [jnp.minimum(f, cap - w) for f in floors]
        for e in range(n_exp):
            window_copy(e, first_rows[e], yextra, sem_extra).start()
        for e in range(n_exp):
            window_copy(e, first_rows[e], yextra, sem_extra).wait()
        o_ref[...] += jnp.dot(onehot(first_rows, floors), yextra[...], preferred_element_type=F32)
        return c

    lax.fori_loop(1, npass, extra_pass, 0)


def _combine(starts, pos_t, x, ye, tm=256):
    n, d = x.shape
    e, cap, _ = ye.shape
    nt = n // tm
    grid_spec = pltpu.PrefetchScalarGridSpec(
        num_scalar_prefetch=1,
        grid=(nt,),
        in_specs=[pl.BlockSpec((tm, e), lambda i, s: (i, 0)),
                  pl.BlockSpec((tm, d), lambda i, s: (i, 0)),
                  pl.BlockSpec(memory_space=pl.ANY)],
        out_specs=pl.BlockSpec((tm, d), lambda i, s: (i, 0)),
        scratch_shapes=[pltpu.VMEM((2, e * COMBINE_WINDOW, d), BF16), pltpu.VMEM((e * COMBINE_WINDOW, d), BF16),
                        pltpu.SemaphoreType.DMA((2,)), pltpu.SemaphoreType.DMA(())],
    )
    return pl.pallas_call(
        functools.partial(_combine_kernel, tm=tm, cap=cap, nt=nt),
        grid_spec=grid_spec,
        out_shape=jax.ShapeDtypeStruct((n, d), F32),
        name="expert_combine",
        compiler_params=_cparams("arbitrary"),
    )(starts.reshape(-1), pos_t, x, ye)


def _expert_choice_ffn(x, g_ffn, w_router_t, w_gate, w_up, w_down, tm=256):
    n, _ = x.shape
    e = w_router_t.shape[0]
    cap = EC_CAPACITY * n // e
    h, aff = _norm_router(x, g_ffn, w_router_t)
    pos, idx, slot_gate, off = _route(aff.reshape(e, n // LANES, LANES), cap)
    rows_per_tile = tm // LANES
    starts = jnp.concatenate([off[:, ::rows_per_tile, 0], jnp.full((e, 1), cap, I32)], axis=1)
    ye = _expert_ffn(idx.reshape(e * cap), slot_gate, h, w_gate, w_up, w_down, tm=tm)
    return _combine(starts, pos.reshape(e, n).T, x, ye, tm=tm)


def _prepare_layer(l, t, g_mix, w_in, pool_w, pool_scale, mla_gq, mla_gkv, mla_w_uq, mla_w_ukv, mla_q_norm,
                   mla_k_norm, na_q_norm, na_k_norm, na_rel_bias, b_gate, w_br_pool, w_br_mla, w_br_na, w_out,
                   g_ffn, w_router, w_e_gate, w_e_up, w_e_down):
    d = w_in.shape[1]
    off_cq = POOL_WIDTH
    off_na = off_cq + 2 * MLA_RANK + MLA_ROPE
    off_gate = off_na + 3 * NA_WIDTH
    wl = w_in[l]
    p = {}
    p["g_mix"] = g_mix[l]
    p["w_pool_in"] = wl[:, :off_cq].astype(BF16)
    p["w_mla_in"] = jnp.pad(wl[:, off_cq:off_na], ((0, 0), (0, MLA_SEG - (off_na - off_cq)))).astype(BF16)
    p["w_na_in"] = wl[:, off_na:off_gate].astype(BF16)
    p["w_gate_in"] = wl[:, off_gate:].astype(BF16)
    p["b_gate"] = b_gate[l].reshape(N_BRANCH * d)
    p["pool_w"] = pool_w[l].astype(BF16)
    p["pool_scale"] = pool_scale[l]
    p["gq"] = mla_gq[l].reshape(1, MLA_RANK)
    p["gkv"] = mla_gkv[l].reshape(1, MLA_RANK)
    wq = mla_w_uq[l].reshape(MLA_RANK, MLA_HEADS, MLA_QK)
    wq_rope = jnp.pad(wq[:, :, MLA_NOPE:], ((0, 0), (0, 0), (0, LANES - MLA_ROPE)))
    p["wuq"] = jnp.concatenate([wq[:, :, :MLA_NOPE].reshape(MLA_RANK, -1), wq_rope.reshape(MLA_RANK, -1)],
                               axis=1).astype(BF16)
    wkv = mla_w_ukv[l].reshape(MLA_RANK, MLA_HEADS, MLA_NOPE + MLA_V)
    p["wukv"] = jnp.concatenate([wkv[:, :, :MLA_NOPE].reshape(MLA_RANK, -1), wkv[:, :, MLA_NOPE:].reshape(MLA_RANK, -1)],
                                axis=1).astype(BF16)
    pad_norm = lambda g: jnp.pad(g, (0, 2 * LANES - MLA_QK)).reshape(1, 2 * LANES)
    p["qn"] = pad_norm(mla_q_norm[l] * (MLA_QK ** -0.5 * LOG2E))
    p["kn"] = pad_norm(mla_k_norm[l])
    p["na_qn"] = jnp.tile(na_q_norm[l] * (NA_HEAD_DIM ** -0.5 * LOG2E), 2).reshape(1, LANES)
    p["na_kn"] = jnp.tile(na_k_norm[l], 2).reshape(1, LANES)
    p["na_bias"] = _na_bias_tables(na_rel_bias[l], t // GRID_W)
    p["w_br_pool"] = w_br_pool[l].astype(BF16)
    p["w_br_mla"] = w_br_mla[l].astype(BF16)
    p["w_br_na"] = w_br_na[l].astype(BF16)
    p["w_out"] = w_out[l].astype(BF16)
    p["g_ffn"] = g_ffn[l]
    p["w_router_t"] = w_router[l].T
    p["w_e_gate"] = w_e_gate[l].astype(BF16)
    p["w_e_up"] = w_e_up[l].astype(BF16)
    p["w_e_down"] = w_e_down[l].astype(BF16)
    return p


def _rope_tables(t):
    half = MLA_ROPE // 2
    pos = jnp.arange(t, dtype=F32)
    inv = 1.0 / (ROPE_THETA ** (jnp.arange(0, MLA_ROPE, 2, dtype=F32) / MLA_ROPE))
    ang = pos[:, None] * inv[None, :]
    cos = jnp.cos(ang)
    sin = jnp.sin(ang)
    zeros = jnp.zeros((t, LANES - MLA_ROPE), F32)
    return jnp.concatenate([cos, cos, zeros], axis=1), jnp.concatenate([-sin, sin, zeros], axis=1)


def _layer(x, p, cos_t, sin_t, b, t):
    m, d = x.shape
    h = _rmsnorm(x, p["g_mix"])
    u = _matmul(h, p["w_pool_in"], F32, tn=POOL_WIDTH, name="in_proj_pool")
    mla_seg = _matmul(h, p["w_mla_in"], F32, tn=MLA_SEG, name="in_proj_mla")
    na_qkv = _matmul(h, p["w_na_in"], F32, tn=1024, name="in_proj_na")
    gates = _matmul(h, p["w_gate_in"], BF16, tn=1024, name="in_proj_gates", bias=p["b_gate"])
    a_pool = _pool_mixer(u.reshape(b, t, POOL_WIDTH), p["pool_w"], p["pool_scale"]).reshape(m, POOL_WIDTH)
    q, k, v = _mla_prep(mla_seg, p["gq"], p["gkv"], p["wuq"], p["wukv"], p["qn"], p["kn"], cos_t, sin_t, b, t)
    a_mla = _mla_attn(q, k, v).reshape(m, MLA_HEADS * MLA_V)
    a_na = _na_attn(na_qkv.reshape(b, t, 3 * NA_WIDTH), p["na_qn"], p["na_kn"], p["na_bias"], b, t).reshape(m, NA_WIDTH)
    merged = _merge(a_pool, a_mla, a_na, p["w_br_pool"], p["w_br_mla"], p["w_br_na"], gates)
    x = _matmul(merged, p["w_out"], F32, tn=1024, name="out_proj", residual=x)
    return _expert_choice_ffn(x, p["g_ffn"], p["w_router_t"], p["w_e_gate"], p["w_e_up"], p["w_e_down"])


def _trunk(x, layers, cos_t, sin_t):
    b, t, d = x.shape
    x = x.reshape(b * t, d)
    for p in layers:
        x = _layer(x, p, cos_t, sin_t, b, t)
    return x.reshape(b, t, d)


def kernel(x_prompt, x_sample, g_mix, w_in, pool_w, pool_scale, mla_gq, mla_gkv, mla_w_uq, mla_w_ukv, mla_q_norm,
           mla_k_norm, na_q_norm, na_k_norm, na_rel_bias, b_gate, w_br_pool, w_br_mla, w_br_na, w_out, g_ffn,
           w_router, w_e_gate, w_e_up, w_e_down):
    weights = (g_mix, w_in, pool_w, pool_scale, mla_gq, mla_gkv, mla_w_uq, mla_w_ukv, mla_q_norm, mla_k_norm,
               na_q_norm, na_k_norm, na_rel_bias, b_gate, w_br_pool, w_br_mla, w_br_na, w_out, g_ffn, w_router,
               w_e_gate, w_e_up, w_e_down)
    prepared = {}
    outs = []
    for x in (x_prompt, x_sample):
        t = x.shape[1]
        if t not in prepared:
            prepared[t] = ([_prepare_layer(l, t, *weights) for l in range(g_mix.shape[0])], _rope_tables(t))
        layers, (cos_t, sin_t) = prepared[t]
        outs.append(_trunk(x, layers, cos_t, sin_t))
    return tuple(outs)
```

```python
import functools
import math

import numpy as np
import jax
import jax.numpy as jnp
from jax import lax
from jax.experimental import pallas as pl
from jax.experimental.pallas import tpu as pltpu

F32 = jnp.float32
BF16 = jnp.bfloat16
I32 = jnp.int32

LANES = 128
VMEM_LIMIT = 56 * 1024 * 1024

GRID_W = 64
POOL_GROUPS = 4
POOL_GROUP_DIM = 256
POOL_WIDTH = POOL_GROUPS * POOL_GROUP_DIM
POOL_WINDOWS = (2, 4, 8, 16)
POOL_HALO = 8
MLA_HEADS = 16
MLA_RANK = 512
MLA_NOPE = 128
MLA_ROPE = 64
MLA_QK = MLA_NOPE + MLA_ROPE
MLA_QK_PAD = 256
MLA_V = 128
MLA_SEG = 2 * MLA_RANK + LANES
ROPE_THETA = 10000.0
NA_HEADS = 16
NA_HEAD_DIM = 64
NA_WIDTH = NA_HEADS * NA_HEAD_DIM
NA_KH = 8
NA_KW = 16
NA_QROWS = 4
NA_KROWS = NA_QROWS + NA_KH
N_BRANCH = 3
N_EXPERTS = 16
EC_CAPACITY = 2
EPS = 1e-6
NEG = -1e30
LOG2E = math.log2(math.e)


def _cparams(*sem):
    return pltpu.CompilerParams(dimension_semantics=sem, vmem_limit_bytes=VMEM_LIMIT)


def _rmsnorm_kernel(x_ref, g_ref, o_ref):
    x = x_ref[...]
    y = x * lax.rsqrt(jnp.mean(x * x, axis=-1, keepdims=True) + EPS) * g_ref[...]
    o_ref[...] = y.astype(o_ref.dtype)


def _rmsnorm(x, g, tm=512):
    m, d = x.shape
    return pl.pallas_call(
        _rmsnorm_kernel,
        grid=(m // tm,),
        in_specs=[pl.BlockSpec((tm, d), lambda i: (i, 0)), pl.BlockSpec((1, d), lambda i: (0, 0))],
        out_specs=pl.BlockSpec((tm, d), lambda i: (i, 0)),
        out_shape=jax.ShapeDtypeStruct((m, d), BF16),
        name="rmsnorm",
        compiler_params=_cparams("parallel"),
    )(x, g.reshape(1, d))


def _norm_router_kernel(x_ref, g_ref, wr_ref, h_ref, aff_ref):
    x = x_ref[...]
    y = x * lax.rsqrt(jnp.mean(x * x, axis=-1, keepdims=True) + EPS) * g_ref[...]
    h_ref[...] = y
    logits = lax.dot_general(wr_ref[...], y, (((1,), (1,)), ((), ())), precision=lax.Precision.HIGHEST,
                             preferred_element_type=F32)
    mx = jnp.max(logits, axis=0, keepdims=True)
    ex = jnp.exp(logits - mx)
    aff_ref[...] = ex / jnp.sum(ex, axis=0, keepdims=True)


def _norm_router(x, g, w_router_t, tm=512):
    m, d = x.shape
    e = w_router_t.shape[0]
    return pl.pallas_call(
        _norm_router_kernel,
        grid=(m // tm,),
        in_specs=[pl.BlockSpec((tm, d), lambda i: (i, 0)), pl.BlockSpec((1, d), lambda i: (0, 0)),
                  pl.BlockSpec((e, d), lambda i: (0, 0))],
        out_specs=[pl.BlockSpec((tm, d), lambda i: (i, 0)), pl.BlockSpec((e, tm), lambda i: (0, i))],
        out_shape=[jax.ShapeDtypeStruct((m, d), F32), jax.ShapeDtypeStruct((e, m), F32)],
        name="norm_router",
        compiler_params=_cparams("parallel"),
    )(x, g.reshape(1, d), w_router_t)


def _mm_kernel(a_ref, w_ref, *rest, epilogue):
    acc = jnp.dot(a_ref[...], w_ref[...], preferred_element_type=F32)
    if epilogue == "sigmoid_bias":
        b_ref, o_ref = rest
        acc = jax.nn.sigmoid(acc + b_ref[...])
    elif epilogue == "residual":
        r_ref, o_ref = rest
        acc = acc + r_ref[...]
    else:
        (o_ref,) = rest
    o_ref[...] = acc.astype(o_ref.dtype)


def _matmul(a, w, out_dtype, tn, name, tm=1024, bias=None, residual=None):
    m, k = a.shape
    n = w.shape[1]
    in_specs = [pl.BlockSpec((tm, k), lambda j, i: (i, 0)), pl.BlockSpec((k, tn), lambda j, i: (0, j))]
    args = [a, w]
    epilogue = "none"
    if bias is not None:
        epilogue = "sigmoid_bias"
        in_specs.append(pl.BlockSpec((1, tn), lambda j, i: (0, j)))
        args.append(bias.reshape(1, n))
    if residual is not None:
        epilogue = "residual"
        in_specs.append(pl.BlockSpec((tm, tn), lambda j, i: (i, j)))
        args.append(residual)
    return pl.pallas_call(
        functools.partial(_mm_kernel, epilogue=epilogue),
        grid=(n // tn, m // tm),
        in_specs=in_specs,
        out_specs=pl.BlockSpec((tm, tn), lambda j, i: (i, j)),
        out_shape=jax.ShapeDtypeStruct((m, n), out_dtype),
        name=name,
        compiler_params=_cparams("parallel", "parallel"),
    )(*args)


def _pool_kernel(u_ref, w_ref, s_ref, o_ref, pad_ref, *, seq, chunk):
    g = pl.program_id(1)
    zeros = jnp.zeros((POOL_HALO, POOL_GROUP_DIM), F32)
    pad_ref[0:POOL_HALO, :] = zeros
    pad_ref[POOL_HALO + seq:2 * POOL_HALO + seq, :] = zeros
    pad_ref[POOL_HALO:POOL_HALO + seq, :] = u_ref[0]

    def window(w):
        left = w // 2
        right = w - 1 - left
        for c in range(seq // chunk):
            base = POOL_HALO + c * chunk
            acc = pad_ref[base - left:base - left + chunk, :]
            for d in range(-left + 1, right + 1):
                acc = acc + pad_ref[base + d:base + d + chunk, :]
            t = c * chunk + lax.broadcasted_iota(I32, (chunk, 1), 0)
            cnt = (jnp.minimum(t + right, seq - 1) + 1 - jnp.maximum(t - left, 0)).astype(F32)
            pooled = acc / cnt - pad_ref[base:base + chunk, :]
            mixed = jnp.dot(pooled.astype(BF16), w_ref[0], preferred_element_type=F32) * s_ref[...]
            o_ref[0, c * chunk:(c + 1) * chunk, :] = mixed.astype(o_ref.dtype)

    for k, w in enumerate(POOL_WINDOWS):
        pl.when(g == k)(functools.partial(window, w))


def _pool_mixer(u, pool_w, pool_scale):
    b, t, _ = u.shape
    chunk = min(512, t)
    return pl.pallas_call(
        functools.partial(_pool_kernel, seq=t, chunk=chunk),
        grid=(b, POOL_GROUPS),
        in_specs=[pl.BlockSpec((1, t, POOL_GROUP_DIM), lambda i, g: (i, 0, g)),
                  pl.BlockSpec((1, POOL_GROUP_DIM, POOL_GROUP_DIM), lambda i, g: (g, 0, 0)),
                  pl.BlockSpec((1, POOL_GROUP_DIM), lambda i, g: (0, g))],
        out_specs=pl.BlockSpec((1, t, POOL_GROUP_DIM), lambda i, g: (i, 0, g)),
        out_shape=jax.ShapeDtypeStruct((b, t, POOL_WIDTH), BF16),
        scratch_shapes=[pltpu.VMEM((t + 2 * POOL_HALO, POOL_GROUP_DIM), F32)],
        name="pool_mixer",
        compiler_params=_cparams("parallel", "parallel"),
    )(u, pool_w, pool_scale.reshape(1, POOL_WIDTH))


def _mla_prep_kernel(seg_ref, gq_ref, gkv_ref, wuq_ref, wuk_ref, wvt_ref, qn_ref, kn_ref, cos_ref, sin_ref,
                     q_ref, k_ref, vt_ref):
    def rms(x):
        return x * lax.rsqrt(jnp.mean(x * x, axis=-1, keepdims=True) + EPS)

    def rope(x):
        lane = lax.broadcasted_iota(I32, x.shape, 1)
        partner = jnp.where(lane < MLA_ROPE // 2, pltpu.roll(x, LANES - MLA_ROPE // 2, 1),
                            pltpu.roll(x, MLA_ROPE // 2, 1))
        return x * cos_ref[...] + partner * sin_ref[...]

    cq = (rms(seg_ref[:, 0:MLA_RANK]) * gq_ref[...]).astype(BF16)
    ckv = (rms(seg_ref[:, MLA_RANK:2 * MLA_RANK]) * gkv_ref[...]).astype(BF16)
    kr = seg_ref[:, 2 * MLA_RANK:2 * MLA_RANK + LANES]
    q = jnp.dot(cq, wuq_ref[...], preferred_element_type=F32)
    kn = jnp.dot(ckv, wuk_ref[...], preferred_element_type=F32)
    vt = lax.dot_general(wvt_ref[...], ckv, (((1,), (1,)), ((), ())), preferred_element_type=F32)
    hn = MLA_HEADS * MLA_NOPE
    kr_sq = kr * kr
    kr_rot = rope(kr * kn_ref[:, LANES:2 * LANES])
    inv_d = 1.0 / MLA_QK
    for h in range(MLA_HEADS):
        qa = q[:, h * LANES:(h + 1) * LANES]
        qb = q[:, hn + h * LANES:hn + (h + 1) * LANES]
        rs = lax.rsqrt(jnp.sum(qa * qa + qb * qb, axis=-1, keepdims=True) * inv_d + EPS)
        q_ref[0, h, :, 0:LANES] = (qa * rs * qn_ref[:, 0:LANES]).astype(BF16)
        q_ref[0, h, :, LANES:2 * LANES] = rope(qb * rs * qn_ref[:, LANES:2 * LANES]).astype(BF16)
        ka = kn[:, h * LANES:(h + 1) * LANES]
        rs = lax.rsqrt(jnp.sum(ka * ka + kr_sq, axis=-1, keepdims=True) * inv_d + EPS)
        k_ref[0, h, :, 0:LANES] = (ka * rs * kn_ref[:, 0:LANES]).astype(BF16)
        k_ref[0, h, :, LANES:2 * LANES] = (kr_rot * rs).astype(BF16)
        vt_ref[0, h, :, :] = vt[h * MLA_V:(h + 1) * MLA_V, :].astype(BF16)


def _mla_prep(seg, gq, gkv, wuq, wuk, wvt, qn, kn, cos_t, sin_t, b, t, tm=256):
    m = b * t
    tpb = t // tm
    const = lambda i: (0, 0)
    head_map = lambda i: (i // tpb, 0, i % tpb, 0)
    return pl.pallas_call(
        _mla_prep_kernel,
        grid=(m // tm,),
        in_specs=[pl.BlockSpec((tm, MLA_SEG), lambda i: (i, 0)),
                  pl.BlockSpec((1, MLA_RANK), const), pl.BlockSpec((1, MLA_RANK), const),
                  pl.BlockSpec(wuq.shape, const), pl.BlockSpec(wuk.shape, const), pl.BlockSpec(wvt.shape, const),
                  pl.BlockSpec((1, 2 * LANES), const), pl.BlockSpec((1, 2 * LANES), const),
                  pl.BlockSpec((tm, LANES), lambda i: (i % tpb, 0)), pl.BlockSpec((tm, LANES), lambda i: (i % tpb, 0))],
        out_specs=[pl.BlockSpec((1, MLA_HEADS, tm, MLA_QK_PAD), head_map),
                   pl.BlockSpec((1, MLA_HEADS, tm, MLA_QK_PAD), head_map),
                   pl.BlockSpec((1, MLA_HEADS, MLA_V, tm), lambda i: (i // tpb, 0, 0, i % tpb))],
        out_shape=[jax.ShapeDtypeStruct((b, MLA_HEADS, t, MLA_QK_PAD), BF16),
                   jax.ShapeDtypeStruct((b, MLA_HEADS, t, MLA_QK_PAD), BF16),
                   jax.ShapeDtypeStruct((b, MLA_HEADS, MLA_V, t), BF16)],
        name="mla_prep",
        compiler_params=_cparams("parallel"),
    )(seg, gq, gkv, wuq, wuk, wvt, qn, kn, cos_t, sin_t)


def _mla_attn_kernel(q_ref, k_ref, vt_ref, o_ref, *, sub, kc):
    n_sub = q_ref.shape[2] // sub
    n_kc = k_ref.shape[2] // kc

    def score_chunk(r, c):
        q = q_ref[0, 0, r * sub:(r + 1) * sub, :]
        k = k_ref[0, 0, c * kc:(c + 1) * kc, :]
        return lax.dot_general(k, q, (((1,), (1,)), ((), ())), preferred_element_type=F32)

    st = [score_chunk(0, c) for c in range(n_kc)]
    for r in range(n_sub):
        m = jnp.max(functools.reduce(jnp.maximum, st), axis=0, keepdims=True)
        nxt, l, ot = [], None, None
        for c in range(n_kc):
            if r + 1 < n_sub:
                nxt.append(score_chunk(r + 1, c))
            pt = jnp.exp2(st[c] - m)
            lc = jnp.sum(pt, axis=0, keepdims=True)
            oc = jnp.dot(vt_ref[0, 0, :, c * kc:(c + 1) * kc], pt.astype(BF16), preferred_element_type=F32)
            l = lc if l is None else l + lc
            ot = oc if ot is None else ot + oc
        o_ref[0, r * sub:(r + 1) * sub, :] = (ot / l).T.astype(o_ref.dtype)
        st = nxt


def _mla_attn(q, k, vt, tq=2048, sub=256, kc=2048):
    b, h, t, _ = q.shape
    tq, kc = min(tq, t), min(kc, t)
    return pl.pallas_call(
        functools.partial(_mla_attn_kernel, sub=sub, kc=kc),
        grid=(b, h, t // tq),
        in_specs=[pl.BlockSpec((1, 1, tq, MLA_QK_PAD), lambda i, j, n: (i, j, n, 0)),
                  pl.BlockSpec((1, 1, t, MLA_QK_PAD), lambda i, j, n: (i, j, 0, 0)),
                  pl.BlockSpec((1, 1, MLA_V, t), lambda i, j, n: (i, j, 0, 0))],
        out_specs=pl.BlockSpec((1, tq, MLA_V), lambda i, j, n: (i, n, j)),
        out_shape=jax.ShapeDtypeStruct((b, t, h * MLA_V), BF16),
        name="mla_attn",
        compiler_params=_cparams("parallel", "parallel", "parallel"),
    )(q, k, vt)


def _na_bias_tables(rel_bias, rows):
    kw = NA_KW
    exact = lax.Precision.HIGHEST
    qc = np.arange(GRID_W)[:, None]
    kc = np.arange(GRID_W)[None, :]
    sc = np.clip(qc - kw // 2, 0, GRID_W - kw)
    valid_c = (kc >= sc) & (kc < sc + kw)
    dx = np.clip(kc - qc + kw - 1, 0, 2 * kw - 2)
    onehot_x = (dx[:, :, None] == np.arange(2 * kw - 1)) & valid_c[:, :, None]
    by_col = jnp.einsum("hyx,qkx->hyqk", rel_bias, jnp.asarray(onehot_x, F32), precision=exact)
    qr = np.arange(NA_QROWS)[:, None]
    kr = np.arange(NA_KROWS)[None, :]
    tables = []
    for r0 in (0, NA_QROWS, rows - NA_QROWS):
        ks = int(np.clip(r0 - NA_KH // 2, 0, rows - NA_KROWS))
        qrow = r0 + qr
        krow = ks + kr
        sr = np.clip(qrow - NA_KH // 2, 0, rows - NA_KH)
        valid_r = (krow >= sr) & (krow < sr + NA_KH)
        dy = np.clip(krow - qrow + NA_KH - 1, 0, 2 * NA_KH - 2)
        onehot_y = (dy[:, :, None] == np.arange(2 * NA_KH - 1)) & valid_r[:, :, None]
        tbl = jnp.einsum("hyqk,rsy->hrqsk", by_col, jnp.asarray(onehot_y, F32), precision=exact)
        valid = valid_r[:, None, :, None] & valid_c[None, :, None, :]
        tbl = jnp.where(valid[None], tbl * LOG2E, NEG)
        tables.append(tbl.reshape(-1, NA_QROWS * GRID_W, NA_KROWS * GRID_W))
    return jnp.stack(tables, axis=1)


def _na_kernel(q_ref, k_ref, v_ref, qn_ref, kn_ref, bias_ref, o_ref, kn_s, v_s, *, rows, chunk, subs, sub_q):
    rb = pl.program_id(2)
    seq = rows * GRID_W
    inv_d = 1.0 / NA_HEAD_DIM

    def head_norm(x, g):
        lane = lax.broadcasted_iota(I32, x.shape, 1)
        lo = lane < NA_HEAD_DIM
        sq = x * x
        s0 = jnp.sum(jnp.where(lo, sq, 0.0), axis=-1, keepdims=True)
        s1 = jnp.sum(jnp.where(lo, 0.0, sq), axis=-1, keepdims=True)
        rs = lax.rsqrt(jnp.where(lo, s0, s1) * inv_d + EPS)
        return x * rs * g

    @pl.when(rb == 0)
    def _():
        for c in range(seq // chunk):
            sl = slice(c * chunk, (c + 1) * chunk)
            kn_s[sl, :] = head_norm(k_ref[0, sl, :], kn_ref[...]).astype(BF16)
            v_s[sl, :] = v_ref[0, sl, :].astype(BF16)

    qn = head_norm(q_ref[0], qn_ref[...])
    lane = lax.broadcasted_iota(I32, (sub_q, LANES), 1)
    n_blocks = rows // NA_QROWS
    units = [(j, a) for j in range(subs) for a in range(2)]

    def window(j):
        blk = rb * subs + j
        ks = jnp.clip(blk * NA_QROWS - NA_KH // 2, 0, rows - NA_KROWS)
        ty = jnp.where(blk == 0, 0, jnp.where(blk == n_blocks - 1, 2, 1))
        return pl.ds(pl.multiple_of(ks * GRID_W, GRID_W), NA_KROWS * GRID_W), ty

    def scores(j, a):
        win, ty = window(j)
        mine = lane < NA_HEAD_DIM if a == 0 else lane >= NA_HEAD_DIM
        qa = jnp.where(mine, qn[j * sub_q:(j + 1) * sub_q, :], 0.0).astype(BF16)
        s = lax.dot_general(qa, kn_s[win, :], (((1,), (1,)), ((), ())), preferred_element_type=F32)
        return s + bias_ref[a, ty]

    outs = {}
    s = scores(*units[0])
    for u, (j, a) in enumerate(units):
        nxt = scores(*units[u + 1]) if u + 1 < len(units) else None
        m = jnp.max(s, axis=-1, keepdims=True)
        p = jnp.exp2(s - m)
        l = jnp.sum(p, axis=-1, keepdims=True)
        outs[j, a] = jnp.dot(p.astype(BF16), v_s[window(j)[0], :], preferred_element_type=F32) / l
        s = nxt
    for j in range(subs):
        merged = jnp.where(lane < NA_HEAD_DIM, outs[j, 0], outs[j, 1])
        o_ref[0, j * sub_q:(j + 1) * sub_q, :] = merged.astype(o_ref.dtype)


def _na_attn(qkv, qn, kn, bias_tables, b, t, subs=2):
    rows = t // GRID_W
    nhp = NA_WIDTH // LANES
    sub_q = NA_QROWS * GRID_W
    tq = subs * sub_q
    nrb = t // tq
    tkw = NA_KROWS * GRID_W
    return pl.pallas_call(
        functools.partial(_na_kernel, rows=rows, chunk=min(512, t), subs=subs, sub_q=sub_q),
        grid=(nhp, b, nrb),
        in_specs=[pl.BlockSpec((1, tq, LANES), lambda hp, i, r: (i, r, hp)),
                  pl.BlockSpec((1, t, LANES), lambda hp, i, r: (i, 0, nhp + hp)),
                  pl.BlockSpec((1, t, LANES), lambda hp, i, r: (i, 0, 2 * nhp + hp)),
                  pl.BlockSpec((1, LANES), lambda hp, i, r: (0, 0)),
                  pl.BlockSpec((1, LANES), lambda hp, i, r: (0, 0)),
                  pl.BlockSpec((2, 3, sub_q, tkw), lambda hp, i, r: (hp, 0, 0, 0))],
        out_specs=pl.BlockSpec((1, tq, LANES), lambda hp, i, r: (i, r, hp)),
        out_shape=jax.ShapeDtypeStruct((b, t, NA_WIDTH), BF16),
        scratch_shapes=[pltpu.VMEM((t, LANES), BF16), pltpu.VMEM((t, LANES), BF16)],
        name="na_attn",
        compiler_params=_cparams("parallel", "parallel", "arbitrary"),
    )(qkv, qkv, qkv, qn, kn, bias_tables)


def _merge_kernel(ap_ref, am_ref, an_ref, wp_ref, wm_ref, wn_ref, g0_ref, g1_ref, g2_ref, o_ref):
    acc = g0_ref[...].astype(F32) * jnp.dot(ap_ref[...], wp_ref[...], preferred_element_type=F32)
    acc = acc + g1_ref[...].astype(F32) * jnp.dot(am_ref[...], wm_ref[...], preferred_element_type=F32)
    acc = acc + g2_ref[...].astype(F32) * jnp.dot(an_ref[...], wn_ref[...], preferred_element_type=F32)
    o_ref[...] = acc.astype(o_ref.dtype)


def _merge(a_pool, a_mla, a_na, w_pool, w_mla, w_na, gates, tm=1024, tn=512):
    m = a_pool.shape[0]
    d = w_pool.shape[1]
    nj = d // tn
    a_spec = lambda a: pl.BlockSpec((tm, a.shape[1]), lambda j, i: (i, 0))
    w_spec = lambda w: pl.BlockSpec((w.shape[0], tn), lambda j, i: (0, j))
    g_spec = lambda br: pl.BlockSpec((tm, tn), lambda j, i: (i, br * nj + j))
    return pl.pallas_call(
        _merge_kernel,
        grid=(nj, m // tm),
        in_specs=[a_spec(a_pool), a_spec(a_mla), a_spec(a_na), w_spec(w_pool), w_spec(w_mla), w_spec(w_na),
                  g_spec(0), g_spec(1), g_spec(2)],
        out_specs=pl.BlockSpec((tm, tn), lambda j, i: (i, j)),
        out_shape=jax.ShapeDtypeStruct((m, d), BF16),
        name="branch_merge",
        compiler_params=_cparams("parallel", "parallel"),
    )(a_pool, a_mla, a_na, w_pool, w_mla, w_na, gates, gates, gates)


def _route_kernel(aff_ref, pos_ref, idx_ref, gate_ref, off_ref, *, cap, idx_chunk):
    a = aff_ref[0]
    r = a.shape[0]
    n = r * LANES
    bits = lax.bitcast_convert_type(a, I32)
    tok = lax.broadcasted_iota(I32, (r, LANES), 0) * LANES + lax.broadcasted_iota(I32, (r, LANES), 1)

    def count(mask):
        return jnp.sum(jnp.sum(mask.astype(F32), axis=1, keepdims=True), axis=0, keepdims=True)

    thr = jnp.zeros((1, 1), I32)
    for bit in range(30, -1, -1):
        cand = thr | (1 << bit)
        thr = jnp.where(count(bits >= cand) >= cap, cand, thr)
    above = bits > thr
    tied = bits == thr
    need = cap - count(above)
    last = jnp.zeros((1, 1), I32)
    for bit in range(int(math.log2(n)) - 1, -1, -1):
        cand = last | (1 << bit)
        last = jnp.where(count(tied & (tok < cand)) < need, cand, last)
    sel = above | (tied & (tok <= last))
    m = sel.astype(BF16)

    li = lax.broadcasted_iota(I32, (LANES, LANES), 0)
    lj = lax.broadcasted_iota(I32, (LANES, LANES), 1)
    upper = (li <= lj).astype(BF16)
    c1 = jnp.dot(m, upper, preferred_element_type=F32)
    rowtot = jnp.broadcast_to(c1[:, LANES - 1:LANES], (r, LANES))
    ri = lax.broadcasted_iota(I32, (r, r), 0)
    rj = lax.broadcasted_iota(I32, (r, r), 1)
    rowoff = jnp.dot((rj < ri).astype(BF16), rowtot.astype(BF16), preferred_element_type=F32)
    pos_ref[0] = jnp.where(sel, c1 - 1.0 + rowoff, -1.0).astype(I32)
    off_ref[0] = rowoff.astype(I32)

    tot_l = lax.dot_general(jnp.ones((8, LANES), BF16), m, (((1,), (1,)), ((), ())),
                            preferred_element_type=F32)
    cum_l = jnp.dot(tot_l.astype(BF16), (ri <= rj).astype(BF16), preferred_element_type=F32)[0:1, :]
    excl_l = cum_l - tot_l[0:1, :]
    lane_r = lax.broadcasted_iota(I32, (idx_chunk, r), 1)
    lane_t = lax.broadcasted_iota(I32, (idx_chunk, LANES), 1)
    for c in range(cap // idx_chunk):
        p = (c * idx_chunk + lax.broadcasted_iota(I32, (idx_chunk, 1), 0)).astype(F32)
        row = jnp.sum((cum_l <= p).astype(F32), axis=1, keepdims=True)
        onehot = lane_r == row.astype(I32)
        off = jnp.sum(jnp.where(onehot, excl_l, 0.0), axis=1, keepdims=True)
        mrow = jnp.dot(onehot.astype(BF16), m, preferred_element_type=F32)
        crow = jnp.dot(mrow.astype(BF16), upper, preferred_element_type=F32)
        lane = jnp.sum((crow <= p - off).astype(F32), axis=1, keepdims=True)
        arow = jnp.dot(onehot.astype(F32), a, precision=lax.Precision.HIGHEST, preferred_element_type=F32)
        sl = slice(c * idx_chunk, (c + 1) * idx_chunk)
        idx_ref[0, sl, :] = (row * LANES + lane).astype(I32)
        gate = jnp.sum(jnp.where(lane_t == lane.astype(I32), arow, 0.0), axis=1, keepdims=True)
        gate_ref[0, sl, :] = jnp.broadcast_to(gate, (idx_chunk, LANES))


def _route(aff3, cap):
    e, r, _ = aff3.shape
    idx_chunk = min(512, cap)
    blk = pl.BlockSpec((1, r, LANES), lambda i: (i, 0, 0))
    slot_blk = pl.BlockSpec((1, cap, 1), lambda i: (i, 0, 0))
    return pl.pallas_call(
        functools.partial(_route_kernel, cap=cap, idx_chunk=idx_chunk),
        grid=(e,),
        in_specs=[blk],
        out_specs=[blk, slot_blk, pl.BlockSpec((1, cap, LANES), lambda i: (i, 0, 0)), blk],
        out_shape=[jax.ShapeDtypeStruct((e, r, LANES), I32), jax.ShapeDtypeStruct((e, cap, 1), I32),
                   jax.ShapeDtypeStruct((e, cap, LANES), F32), jax.ShapeDtypeStruct((e, r, LANES), I32)],
        name="route",
        compiler_params=_cparams("parallel"),
    )(aff3)


def _ffn_kernel(idx_ref, h_hbm, gate_ref, wg_ref, wu_ref, wd_ref, o_ref, xbuf, sem, *, tm):
    nb = pl.num_programs(1)
    step = pl.program_id(0) * nb + pl.program_id(1)
    nsteps = pl.num_programs(0) * nb
    slot = step % 2

    def row_copy(block, slot_, r):
        tok = idx_ref[block * tm + r]
        return pltpu.make_async_copy(h_hbm.at[pl.ds(tok, 1), :], xbuf.at[slot_, pl.ds(r, 1), :], sem.at[slot_])

    def wait_block(slot_):
        pltpu.make_async_copy(h_hbm.at[pl.ds(0, tm), :], xbuf.at[slot_], sem.at[slot_]).wait()

    @pl.when(step == 0)
    def _():
        def issue(r, c):
            row_copy(0, 0, r).start()
            return c
        lax.fori_loop(0, tm, issue, 0, unroll=8)

    wait_block(slot)
    xe = xbuf[slot].astype(BF16)
    nxt = jnp.minimum(step + 1, nsteps - 1)
    for r in range(tm):
        row_copy(nxt, 1 - slot, r).start()
    gate = jnp.dot(xe, wg_ref[0], preferred_element_type=F32)
    up = jnp.dot(xe, wu_ref[0], preferred_element_type=F32)
    hid = (gate * jax.nn.sigmoid(gate) * up).astype(BF16)
    ye = jnp.dot(hid, wd_ref[0], preferred_element_type=F32)
    slot_gate = jnp.concatenate([gate_ref[0]] * (ye.shape[1] // LANES), axis=1)
    o_ref[0] = (ye * slot_gate).astype(o_ref.dtype)

    @pl.when(step == nsteps - 1)
    def _():
        wait_block(1 - slot)


def _expert_ffn(idx, slot_gate, h, w_gate, w_up, w_down, tm=256):
    e, cap, _ = slot_gate.shape
    d = h.shape[1]
    ff = w_gate.shape[2]
    nb = cap // tm
    grid_spec = pltpu.PrefetchScalarGridSpec(
        num_scalar_prefetch=1,
        grid=(e, nb),
        in_specs=[pl.BlockSpec(memory_space=pl.ANY),
                  pl.BlockSpec((1, tm, LANES), lambda i, j, s: (i, j, 0)),
                  pl.BlockSpec((1, d, ff), lambda i, j, s: (i, 0, 0)),
                  pl.BlockSpec((1, d, ff), lambda i, j, s: (i, 0, 0)),
                  pl.BlockSpec((1, ff, d), lambda i, j, s: (i, 0, 0))],
        out_specs=pl.BlockSpec((1, tm, d), lambda i, j, s: (i, j, 0)),
        scratch_shapes=[pltpu.VMEM((2, tm, d), F32), pltpu.SemaphoreType.DMA((2,))],
    )
    return pl.pallas_call(
        functools.partial(_ffn_kernel, tm=tm),
        grid_spec=grid_spec,
        out_shape=jax.ShapeDtypeStruct((e, cap, d), BF16),
        name="expert_ffn",
        compiler_params=_cparams("arbitrary", "arbitrary"),
    )(idx, h, slot_gate, w_gate, w_up, w_down)


COMBINE_WINDOW = 128
SLOT_ALIGN = 16


def _combine_kernel(starts_ref, pos_ref, x_ref, ye_hbm, o_ref, ycat, yextra, sem, sem_extra, *, tm, cap, nt):
    i = pl.program_id(0)
    n_exp = pos_ref.shape[1]
    w = COMBINE_WINDOW
    slot = i % 2

    def base_of(e, tile):
        start = starts_ref[e * (nt + 1) + tile]
        return jnp.minimum((start // SLOT_ALIGN) * SLOT_ALIGN, cap - w)

    def window_copy(e, first_row, dst, dst_sem):
        src = ye_hbm.at[e, pl.ds(pl.multiple_of(first_row, SLOT_ALIGN), w), :]
        return pltpu.make_async_copy(src, dst.at[pl.ds(e * w, w), :], dst_sem)

    def prefetch(tile, slot_):
        for e in range(n_exp):
            window_copy(e, base_of(e, tile), ycat.at[slot_], sem.at[slot_]).start()

    @pl.when(i == 0)
    def _():
        prefetch(0, 0)

    @pl.when(i + 1 < nt)
    def _():
        prefetch(i + 1, 1 - slot)

    bases = [base_of(e, i) for e in range(n_exp)]
    lane = lax.broadcasted_iota(I32, (tm, w), 1)

    def onehot(first_rows, floors):
        cols = []
        for e in range(n_exp):
            pos = pos_ref[:, e:e + 1]
            hit = pos - first_rows[e] == lane
            if floors is not None:
                hit = hit & (pos >= floors[e])
            cols.append(hit.astype(BF16))
        return jnp.concatenate(cols, axis=1)

    sel = onehot(bases, None)
    for e in range(n_exp):
        window_copy(e, bases[e], ycat.at[slot], sem.at[slot]).wait()
    o_ref[...] = x_ref[...] + jnp.dot(sel, ycat[slot], preferred_element_type=F32)

    npass = jnp.int32(0)
    for e in range(n_exp):
        end = starts_ref[e * (nt + 1) + i + 1]
        npass = jnp.maximum(npass, (end - bases[e] + w - 1) // w)

    def extra_pass(j, c):
        floors = [bases[e] + j * w for e in range(n_exp)]
        first_rows = [jnp.minimum(f, cap - w) for f in floors]
        for e in range(n_exp):
            window_copy(e, first_rows[e], yextra, sem_extra).start()
        for e in range(n_exp):
            window_copy(e, first_rows[e], yextra, sem_extra).wait()
        o_ref[...] += jnp.dot(onehot(first_rows, floors), yextra[...], preferred_element_type=F32)
        return c

    lax.fori_loop(1, npass, extra_pass, 0)


def _combine(starts, pos_t, x, ye, tm=256):
    n, d = x.shape
    e, cap, _ = ye.shape
    nt = n // tm
    grid_spec = pltpu.PrefetchScalarGridSpec(
        num_scalar_prefetch=1,
        grid=(nt,),
        in_specs=[pl.BlockSpec((tm, e), lambda i, s: (i, 0)),
                  pl.BlockSpec((tm, d), lambda i, s: (i, 0)),
                  pl.BlockSpec(memory_space=pl.ANY)],
        out_specs=pl.BlockSpec((tm, d), lambda i, s: (i, 0)),
        scratch_shapes=[pltpu.VMEM((2, e * COMBINE_WINDOW, d), BF16), pltpu.VMEM((e * COMBINE_WINDOW, d), BF16),
                        pltpu.SemaphoreType.DMA((2,)), pltpu.SemaphoreType.DMA(())],
    )
    return pl.pallas_call(
        functools.partial(_combine_kernel, tm=tm, cap=cap, nt=nt),
        grid_spec=grid_spec,
        out_shape=jax.ShapeDtypeStruct((n, d), F32),
        name="expert_combine",
        compiler_params=_cparams("arbitrary"),
    )(starts.reshape(-1), pos_t, x, ye)


def _expert_choice_ffn(x, g_ffn, w_router_t, w_gate, w_up, w_down, tm=256):
    n, _ = x.shape
    e = w_router_t.shape[0]
    cap = EC_CAPACITY * n // e
    h, aff = _norm_router(x, g_ffn, w_router_t)
    pos, idx, slot_gate, off = _route(aff.reshape(e, n // LANES, LANES), cap)
    rows_per_tile = tm // LANES
    starts = jnp.concatenate([off[:, ::rows_per_tile, 0], jnp.full((e, 1), cap, I32)], axis=1)
    ye = _expert_ffn(idx.reshape(e * cap), slot_gate, h, w_gate, w_up, w_down, tm=tm)
    return _combine(starts, pos.reshape(e, n).T, x, ye, tm=tm)


def _prepare_layer(l, t, g_mix, w_in, pool_w, pool_scale, mla_gq, mla_gkv, mla_w_uq, mla_w_ukv, mla_q_norm,
                   mla_k_norm, na_q_norm, na_k_norm, na_rel_bias, b_gate, w_br_pool, w_br_mla, w_br_na, w_out,
                   g_ffn, w_router, w_e_gate, w_e_up, w_e_down):
    d = w_in.shape[1]
    off_cq = POOL_WIDTH
    off_na = off_cq + 2 * MLA_RANK + MLA_ROPE
    off_gate = off_na + 3 * NA_WIDTH
    wl = w_in[l]
    p = {}
    p["g_mix"] = g_mix[l]
    p["w_pool_in"] = wl[:, :off_cq].astype(BF16)
    p["w_mla_in"] = jnp.pad(wl[:, off_cq:off_na], ((0, 0), (0, MLA_SEG - (off_na - off_cq)))).astype(BF16)
    p["w_na_in"] = wl[:, off_na:off_gate].astype(BF16)
    p["w_gate_in"] = wl[:, off_gate:].astype(BF16)
    p["b_gate"] = b_gate[l].reshape(N_BRANCH * d)
    p["pool_w"] = pool_w[l].astype(BF16)
    p["pool_scale"] = pool_scale[l]
    p["gq"] = mla_gq[l].reshape(1, MLA_RANK)
    p["gkv"] = mla_gkv[l].reshape(1, MLA_RANK)
    wq = mla_w_uq[l].reshape(MLA_RANK, MLA_HEADS, MLA_QK)
    wq_rope = jnp.pad(wq[:, :, MLA_NOPE:], ((0, 0), (0, 0), (0, LANES - MLA_ROPE)))
    p["wuq"] = jnp.concatenate([wq[:, :, :MLA_NOPE].reshape(MLA_RANK, -1), wq_rope.reshape(MLA_RANK, -1)],
                               axis=1).astype(BF16)
    wkv = mla_w_ukv[l].reshape(MLA_RANK, MLA_HEADS, MLA_NOPE + MLA_V)
    p["wuk"] = wkv[:, :, :MLA_NOPE].reshape(MLA_RANK, -1).astype(BF16)
    p["wvt"] = wkv[:, :, MLA_NOPE:].reshape(MLA_RANK, -1).T.astype(BF16)
    pad_norm = lambda g: jnp.pad(g, (0, 2 * LANES - MLA_QK)).reshape(1, 2 * LANES)
    p["qn"] = pad_norm(mla_q_norm[l] * (MLA_QK ** -0.5 * LOG2E))
    p["kn"] = pad_norm(mla_k_norm[l])
    p["na_qn"] = jnp.tile(na_q_norm[l] * (NA_HEAD_DIM ** -0.5 * LOG2E), 2).reshape(1, LANES)
    p["na_kn"] = jnp.tile(na_k_norm[l], 2).reshape(1, LANES)
    p["na_bias"] = _na_bias_tables(na_rel_bias[l], t // GRID_W)
    p["w_br_pool"] = w_br_pool[l].astype(BF16)
    p["w_br_mla"] = w_br_mla[l].astype(BF16)
    p["w_br_na"] = w_br_na[l].astype(BF16)
    p["w_out"] = w_out[l].astype(BF16)
    p["g_ffn"] = g_ffn[l]
    p["w_router_t"] = w_router[l].T
    p["w_e_gate"] = w_e_gate[l].astype(BF16)
    p["w_e_up"] = w_e_up[l].astype(BF16)
    p["w_e_down"] = w_e_down[l].astype(BF16)
    return p


def _rope_tables(t):
    half = MLA_ROPE // 2
    pos = jnp.arange(t, dtype=F32)
    inv = 1.0 / (ROPE_THETA ** (jnp.arange(0, MLA_ROPE, 2, dtype=F32) / MLA_ROPE))
    ang = pos[:, None] * inv[None, :]
    cos = jnp.cos(ang)
    sin = jnp.sin(ang)
    zeros = jnp.zeros((t, LANES - MLA_ROPE), F32)
    return jnp.concatenate([cos, cos, zeros], axis=1), jnp.concatenate([-sin, sin, zeros], axis=1)


def _layer(x, p, cos_t, sin_t, b, t):
    m, d = x.shape
    h = _rmsnorm(x, p["g_mix"])
    u = _matmul(h, p["w_pool_in"], F32, tn=POOL_WIDTH, name="in_proj_pool")
    mla_seg = _matmul(h, p["w_mla_in"], F32, tn=MLA_SEG, name="in_proj_mla")
    na_qkv = _matmul(h, p["w_na_in"], F32, tn=1024, name="in_proj_na")
    gates = _matmul(h, p["w_gate_in"], BF16, tn=1024, name="in_proj_gates", bias=p["b_gate"])
    a_pool = _pool_mixer(u.reshape(b, t, POOL_WIDTH), p["pool_w"], p["pool_scale"]).reshape(m, POOL_WIDTH)
    q, k, vt = _mla_prep(mla_seg, p["gq"], p["gkv"], p["wuq"], p["wuk"], p["wvt"], p["qn"], p["kn"], cos_t, sin_t,
                         b, t)
    a_mla = _mla_attn(q, k, vt).reshape(m, MLA_HEADS * MLA_V)
    a_na = _na_attn(na_qkv.reshape(b, t, 3 * NA_WIDTH), p["na_qn"], p["na_kn"], p["na_bias"], b, t).reshape(m, NA_WIDTH)
    merged = _merge(a_pool, a_mla, a_na, p["w_br_pool"], p["w_br_mla"], p["w_br_na"], gates)
    x = _matmul(merged, p["w_out"], F32, tn=1024, name="out_proj", residual=x)
    return _expert_choice_ffn(x, p["g_ffn"], p["w_router_t"], p["w_e_gate"], p["w_e_up"], p["w_e_down"])


def _trunk(x, layers, cos_t, sin_t):
    b, t, d = x.shape
    x = x.reshape(b * t, d)
    for p in layers:
        x = _layer(x, p, cos_t, sin_t, b, t)
    return x.reshape(b, t, d)


def kernel(x_prompt, x_sample, g_mix, w_in, pool_w, pool_scale, mla_gq, mla_gkv, mla_w_uq, mla_w_ukv, mla_q_norm,
           mla_k_norm, na_q_norm, na_k_norm, na_rel_bias, b_gate, w_br_pool, w_br_mla, w_br_na, w_out, g_ffn,
           w_router, w_e_gate, w_e_up, w_e_down):
    weights = (g_mix, w_in, pool_w, pool_scale, mla_gq, mla_gkv, mla_w_uq, mla_w_ukv, mla_q_norm, mla_k_norm,
               na_q_norm, na_k_norm, na_rel_bias, b_gate, w_br_pool, w_br_mla, w_br_na, w_out, g_ffn, w_router,
               w_e_gate, w_e_up, w_e_down)
    prepared = {}
    outs = []
    for x in (x_prompt, x_sample):
        t = x.shape[1]
        if t not in prepared:
            prepared[t] = ([_prepare_layer(l, t, *weights) for l in range(g_mix.shape[0])], _rope_tables(t))
        layers, (cos_t, sin_t) = prepared[t]
        outs.append(_trunk(x, layers, cos_t, sin_t))
    return tuple(outs)
```

```python
import functools
import math

import numpy as np
import jax
import jax.numpy as jnp
from jax import lax
from jax.experimental import pallas as pl
from jax.experimental.pallas import tpu as pltpu

F32 = jnp.float32
BF16 = jnp.bfloat16
I32 = jnp.int32

LANES = 128
VMEM_LIMIT = 56 * 1024 * 1024

GRID_W = 64
POOL_GROUPS = 4
POOL_GROUP_DIM = 256
POOL_WIDTH = POOL_GROUPS * POOL_GROUP_DIM
POOL_WINDOWS = (2, 4, 8, 16)
POOL_HALO = 8
MLA_HEADS = 16
MLA_RANK = 512
MLA_NOPE = 128
MLA_ROPE = 64
MLA_QK = MLA_NOPE + MLA_ROPE
MLA_QK_PAD = 256
MLA_V = 128
MLA_SEG = 2 * MLA_RANK + LANES
ROPE_THETA = 10000.0
NA_HEADS = 16
NA_HEAD_DIM = 64
NA_WIDTH = NA_HEADS * NA_HEAD_DIM
NA_KH = 8
NA_KW = 16
NA_QROWS = 4
NA_KROWS = NA_QROWS + NA_KH
N_BRANCH = 3
N_EXPERTS = 16
EC_CAPACITY = 2
EPS = 1e-6
NEG = -1e30
LOG2E = math.log2(math.e)


def _cparams(*sem):
    return pltpu.CompilerParams(dimension_semantics=sem, vmem_limit_bytes=VMEM_LIMIT)


def _rmsnorm_kernel(x_ref, g_ref, o_ref):
    x = x_ref[...]
    y = x * lax.rsqrt(jnp.mean(x * x, axis=-1, keepdims=True) + EPS) * g_ref[...]
    o_ref[...] = y.astype(o_ref.dtype)


def _rmsnorm(x, g, tm=512):
    m, d = x.shape
    return pl.pallas_call(
        _rmsnorm_kernel,
        grid=(m // tm,),
        in_specs=[pl.BlockSpec((tm, d), lambda i: (i, 0)), pl.BlockSpec((1, d), lambda i: (0, 0))],
        out_specs=pl.BlockSpec((tm, d), lambda i: (i, 0)),
        out_shape=jax.ShapeDtypeStruct((m, d), BF16),
        name="rmsnorm",
        compiler_params=_cparams("parallel"),
    )(x, g.reshape(1, d))


def _norm_router_kernel(x_ref, g_ref, wr_ref, h_ref, aff_ref):
    x = x_ref[...]
    y = x * lax.rsqrt(jnp.mean(x * x, axis=-1, keepdims=True) + EPS) * g_ref[...]
    h_ref[...] = y
    logits = lax.dot_general(wr_ref[...], y, (((1,), (1,)), ((), ())), precision=lax.Precision.HIGHEST,
                             preferred_element_type=F32)
    mx = jnp.max(logits, axis=0, keepdims=True)
    ex = jnp.exp(logits - mx)
    aff_ref[...] = ex / jnp.sum(ex, axis=0, keepdims=True)


def _norm_router(x, g, w_router_t, tm=512):
    m, d = x.shape
    e = w_router_t.shape[0]
    return pl.pallas_call(
        _norm_router_kernel,
        grid=(m // tm,),
        in_specs=[pl.BlockSpec((tm, d), lambda i: (i, 0)), pl.BlockSpec((1, d), lambda i: (0, 0)),
                  pl.BlockSpec((e, d), lambda i: (0, 0))],
        out_specs=[pl.BlockSpec((tm, d), lambda i: (i, 0)), pl.BlockSpec((e, tm), lambda i: (0, i))],
        out_shape=[jax.ShapeDtypeStruct((m, d), F32), jax.ShapeDtypeStruct((e, m), F32)],
        name="norm_router",
        compiler_params=_cparams("parallel"),
    )(x, g.reshape(1, d), w_router_t)


def _mm_kernel(a_ref, w_ref, *rest, epilogue):
    acc = jnp.dot(a_ref[...], w_ref[...], preferred_element_type=F32)
    if epilogue == "sigmoid_bias":
        b_ref, o_ref = rest
        acc = jax.nn.sigmoid(acc + b_ref[...])
    elif epilogue == "residual":
        r_ref, o_ref = rest
        acc = acc + r_ref[...]
    else:
        (o_ref,) = rest
    o_ref[...] = acc.astype(o_ref.dtype)


def _matmul(a, w, out_dtype, tn, name, tm=1024, bias=None, residual=None):
    m, k = a.shape
    n = w.shape[1]
    in_specs = [pl.BlockSpec((tm, k), lambda j, i: (i, 0)), pl.BlockSpec((k, tn), lambda j, i: (0, j))]
    args = [a, w]
    epilogue = "none"
    if bias is not None:
        epilogue = "sigmoid_bias"
        in_specs.append(pl.BlockSpec((1, tn), lambda j, i: (0, j)))
        args.append(bias.reshape(1, n))
    if residual is not None:
        epilogue = "residual"
        in_specs.append(pl.BlockSpec((tm, tn), lambda j, i: (i, j)))
        args.append(residual)
    return pl.pallas_call(
        functools.partial(_mm_kernel, epilogue=epilogue),
        grid=(n // tn, m // tm),
        in_specs=in_specs,
        out_specs=pl.BlockSpec((tm, tn), lambda j, i: (i, j)),
        out_shape=jax.ShapeDtypeStruct((m, n), out_dtype),
        name=name,
        compiler_params=_cparams("parallel", "parallel"),
    )(*args)


def _pool_kernel(u_ref, w_ref, s_ref, o_ref, pad_ref, *, seq, chunk):
    g = pl.program_id(1)
    zeros = jnp.zeros((POOL_HALO, POOL_GROUP_DIM), F32)
    pad_ref[0:POOL_HALO, :] = zeros
    pad_ref[POOL_HALO + seq:2 * POOL_HALO + seq, :] = zeros
    pad_ref[POOL_HALO:POOL_HALO + seq, :] = u_ref[0]

    def window(w):
        left = w // 2
        right = w - 1 - left
        for c in range(seq // chunk):
            base = POOL_HALO + c * chunk
            acc = pad_ref[base - left:base - left + chunk, :]
            for d in range(-left + 1, right + 1):
                acc = acc + pad_ref[base + d:base + d + chunk, :]
            t = c * chunk + lax.broadcasted_iota(I32, (chunk, 1), 0)
            cnt = (jnp.minimum(t + right, seq - 1) + 1 - jnp.maximum(t - left, 0)).astype(F32)
            pooled = acc / cnt - pad_ref[base:base + chunk, :]
            mixed = jnp.dot(pooled.astype(BF16), w_ref[0], preferred_element_type=F32) * s_ref[...]
            o_ref[0, c * chunk:(c + 1) * chunk, :] = mixed.astype(o_ref.dtype)

    for k, w in enumerate(POOL_WINDOWS):
        pl.when(g == k)(functools.partial(window, w))


def _pool_mixer(u, pool_w, pool_scale):
    b, t, _ = u.shape
    chunk = min(512, t)
    return pl.pallas_call(
        functools.partial(_pool_kernel, seq=t, chunk=chunk),
        grid=(b, POOL_GROUPS),
        in_specs=[pl.BlockSpec((1, t, POOL_GROUP_DIM), lambda i, g: (i, 0, g)),
                  pl.BlockSpec((1, POOL_GROUP_DIM, POOL_GROUP_DIM), lambda i, g: (g, 0, 0)),
                  pl.BlockSpec((1, POOL_GROUP_DIM), lambda i, g: (0, g))],
        out_specs=pl.BlockSpec((1, t, POOL_GROUP_DIM), lambda i, g: (i, 0, g)),
        out_shape=jax.ShapeDtypeStruct((b, t, POOL_WIDTH), BF16),
        scratch_shapes=[pltpu.VMEM((t + 2 * POOL_HALO, POOL_GROUP_DIM), F32)],
        name="pool_mixer",
        compiler_params=_cparams("parallel", "parallel"),
    )(u, pool_w, pool_scale.reshape(1, POOL_WIDTH))


def _mla_prep_kernel(seg_ref, gq_ref, gkv_ref, wuq_ref, wuk_ref, wvt_ref, qn_ref, kn_ref, cos_ref, sin_ref,
                     q_ref, k_ref, vt_ref):
    def rms(x):
        return x * lax.rsqrt(jnp.mean(x * x, axis=-1, keepdims=True) + EPS)

    def rope(x):
        lane = lax.broadcasted_iota(I32, x.shape, 1)
        partner = jnp.where(lane < MLA_ROPE // 2, pltpu.roll(x, LANES - MLA_ROPE // 2, 1),
                            pltpu.roll(x, MLA_ROPE // 2, 1))
        return x * cos_ref[...] + partner * sin_ref[...]

    cq = (rms(seg_ref[:, 0:MLA_RANK]) * gq_ref[...]).astype(BF16)
    ckv = (rms(seg_ref[:, MLA_RANK:2 * MLA_RANK]) * gkv_ref[...]).astype(BF16)
    kr = seg_ref[:, 2 * MLA_RANK:2 * MLA_RANK + LANES]
    q = jnp.dot(cq, wuq_ref[...], preferred_element_type=F32)
    kn = jnp.dot(ckv, wuk_ref[...], preferred_element_type=F32)
    vt = lax.dot_general(wvt_ref[...], ckv, (((1,), (1,)), ((), ())), preferred_element_type=F32)
    hn = MLA_HEADS * MLA_NOPE
    kr_sq = kr * kr
    kr_rot = rope(kr * kn_ref[:, LANES:2 * LANES])
    inv_d = 1.0 / MLA_QK
    for h in range(MLA_HEADS):
        qa = q[:, h * LANES:(h + 1) * LANES]
        qb = q[:, hn + h * LANES:hn + (h + 1) * LANES]
        rs = lax.rsqrt(jnp.sum(qa * qa + qb * qb, axis=-1, keepdims=True) * inv_d + EPS)
        q_ref[0, h, :, 0:LANES] = (qa * rs * qn_ref[:, 0:LANES]).astype(BF16)
        q_ref[0, h, :, LANES:2 * LANES] = rope(qb * rs * qn_ref[:, LANES:2 * LANES]).astype(BF16)
        ka = kn[:, h * LANES:(h + 1) * LANES]
        rs = lax.rsqrt(jnp.sum(ka * ka + kr_sq, axis=-1, keepdims=True) * inv_d + EPS)
        k_ref[0, h, :, 0:LANES] = (ka * rs * kn_ref[:, 0:LANES]).astype(BF16)
        k_ref[0, h, :, LANES:2 * LANES] = (kr_rot * rs).astype(BF16)
        vt_ref[0, h, :, :] = vt[h * MLA_V:(h + 1) * MLA_V, :].astype(BF16)


def _mla_prep(seg, gq, gkv, wuq, wuk, wvt, qn, kn, cos_t, sin_t, b, t, tm=256):
    m = b * t
    tpb = t // tm
    const = lambda i: (0, 0)
    head_map = lambda i: (i // tpb, 0, i % tpb, 0)
    return pl.pallas_call(
        _mla_prep_kernel,
        grid=(m // tm,),
        in_specs=[pl.BlockSpec((tm, MLA_SEG), lambda i: (i, 0)),
                  pl.BlockSpec((1, MLA_RANK), const), pl.BlockSpec((1, MLA_RANK), const),
                  pl.BlockSpec(wuq.shape, const), pl.BlockSpec(wuk.shape, const), pl.BlockSpec(wvt.shape, const),
                  pl.BlockSpec((1, 2 * LANES), const), pl.BlockSpec((1, 2 * LANES), const),
                  pl.BlockSpec((tm, LANES), lambda i: (i % tpb, 0)), pl.BlockSpec((tm, LANES), lambda i: (i % tpb, 0))],
        out_specs=[pl.BlockSpec((1, MLA_HEADS, tm, MLA_QK_PAD), head_map),
                   pl.BlockSpec((1, MLA_HEADS, tm, MLA_QK_PAD), head_map),
                   pl.BlockSpec((1, MLA_HEADS, MLA_V, tm), lambda i: (i // tpb, 0, 0, i % tpb))],
        out_shape=[jax.ShapeDtypeStruct((b, MLA_HEADS, t, MLA_QK_PAD), BF16),
                   jax.ShapeDtypeStruct((b, MLA_HEADS, t, MLA_QK_PAD), BF16),
                   jax.ShapeDtypeStruct((b, MLA_HEADS, MLA_V, t), BF16)],
        name="mla_prep",
        compiler_params=_cparams("parallel"),
    )(seg, gq, gkv, wuq, wuk, wvt, qn, kn, cos_t, sin_t)


def _mla_attn_kernel(q_ref, k_ref, vt_ref, o_ref, *, sub, kc):
    n_sub = q_ref.shape[2] // sub
    n_kc = k_ref.shape[2] // kc

    def score_chunk(r, c):
        q = q_ref[0, 0, r * sub:(r + 1) * sub, :]
        k = k_ref[0, 0, c * kc:(c + 1) * kc, :]
        return lax.dot_general(k, q, (((1,), (1,)), ((), ())), preferred_element_type=F32)

    st = [score_chunk(0, c) for c in range(n_kc)]
    for r in range(n_sub):
        m = jnp.max(functools.reduce(jnp.maximum, st), axis=0, keepdims=True)
        nxt, l, ot = [], None, None
        for c in range(n_kc):
            if r + 1 < n_sub:
                nxt.append(score_chunk(r + 1, c))
            pt = jnp.exp2(st[c] - m)
            lc = jnp.sum(pt, axis=0, keepdims=True)
            oc = jnp.dot(vt_ref[0, 0, :, c * kc:(c + 1) * kc], pt.astype(BF16), preferred_element_type=F32)
            l = lc if l is None else l + lc
            ot = oc if ot is None else ot + oc
        o_ref[0, r * sub:(r + 1) * sub, :] = (ot / l).T.astype(o_ref.dtype)
        st = nxt


def _mla_attn(q, k, vt, tq=2048, sub=256, kc=2048):
    b, h, t, _ = q.shape
    tq, kc = min(tq, t), min(kc, t)
    return pl.pallas_call(
        functools.partial(_mla_attn_kernel, sub=sub, kc=kc),
        grid=(b, h, t // tq),
        in_specs=[pl.BlockSpec((1, 1, tq, MLA_QK_PAD), lambda i, j, n: (i, j, n, 0)),
                  pl.BlockSpec((1, 1, t, MLA_QK_PAD), lambda i, j, n: (i, j, 0, 0)),
                  pl.BlockSpec((1, 1, MLA_V, t), lambda i, j, n: (i, j, 0, 0))],
        out_specs=pl.BlockSpec((1, tq, MLA_V), lambda i, j, n: (i, n, j)),
        out_shape=jax.ShapeDtypeStruct((b, t, h * MLA_V), BF16),
        name="mla_attn",
        compiler_params=_cparams("parallel", "parallel", "parallel"),
    )(q, k, vt)


def _na_bias_tables(rel_bias, rows):
    kw = NA_KW
    exact = lax.Precision.HIGHEST
    qc = np.arange(GRID_W)[:, None]
    kc = np.arange(GRID_W)[None, :]
    sc = np.clip(qc - kw // 2, 0, GRID_W - kw)
    valid_c = (kc >= sc) & (kc < sc + kw)
    dx = np.clip(kc - qc + kw - 1, 0, 2 * kw - 2)
    onehot_x = (dx[:, :, None] == np.arange(2 * kw - 1)) & valid_c[:, :, None]
    by_col = jnp.einsum("hyx,qkx->hyqk", rel_bias, jnp.asarray(onehot_x, F32), precision=exact)
    qr = np.arange(NA_QROWS)[:, None]
    kr = np.arange(NA_KROWS)[None, :]
    tables = []
    for r0 in (0, NA_QROWS, rows - NA_QROWS):
        ks = int(np.clip(r0 - NA_KH // 2, 0, rows - NA_KROWS))
        qrow = r0 + qr
        krow = ks + kr
        sr = np.clip(qrow - NA_KH // 2, 0, rows - NA_KH)
        valid_r = (krow >= sr) & (krow < sr + NA_KH)
        dy = np.clip(krow - qrow + NA_KH - 1, 0, 2 * NA_KH - 2)
        onehot_y = (dy[:, :, None] == np.arange(2 * NA_KH - 1)) & valid_r[:, :, None]
        tbl = jnp.einsum("hyqk,rsy->hrqsk", by_col, jnp.asarray(onehot_y, F32), precision=exact)
        valid = valid_r[:, None, :, None] & valid_c[None, :, None, :]
        tbl = jnp.where(valid[None], tbl * LOG2E, NEG)
        tables.append(tbl.reshape(-1, NA_QROWS * GRID_W, NA_KROWS * GRID_W))
    return jnp.stack(tables, axis=1)


def _na_kernel(q_ref, k_ref, v_ref, qn_ref, kn_ref, bias_ref, o_ref, kn_s, v_s, *, rows, chunk, subs, sub_q):
    rb = pl.program_id(2)
    seq = rows * GRID_W
    inv_d = 1.0 / NA_HEAD_DIM

    def head_norm(x, g):
        lane = lax.broadcasted_iota(I32, x.shape, 1)
        lo = lane < NA_HEAD_DIM
        sq = x * x
        s0 = jnp.sum(jnp.where(lo, sq, 0.0), axis=-1, keepdims=True)
        s1 = jnp.sum(jnp.where(lo, 0.0, sq), axis=-1, keepdims=True)
        rs = lax.rsqrt(jnp.where(lo, s0, s1) * inv_d + EPS)
        return x * rs * g

    @pl.when(rb == 0)
    def _():
        for c in range(seq // chunk):
            sl = slice(c * chunk, (c + 1) * chunk)
            kn_s[sl, :] = head_norm(k_ref[0, sl, :], kn_ref[...]).astype(BF16)
            v_s[sl, :] = v_ref[0, sl, :].astype(BF16)

    qn = head_norm(q_ref[0], qn_ref[...])
    lane = lax.broadcasted_iota(I32, (sub_q, LANES), 1)
    n_blocks = rows // NA_QROWS
    units = [(j, a) for j in range(subs) for a in range(2)]

    def window(j):
        blk = rb * subs + j
        ks = jnp.clip(blk * NA_QROWS - NA_KH // 2, 0, rows - NA_KROWS)
        ty = jnp.where(blk == 0, 0, jnp.where(blk == n_blocks - 1, 2, 1))
        return pl.ds(pl.multiple_of(ks * GRID_W, GRID_W), NA_KROWS * GRID_W), ty

    def scores(j, a):
        win, ty = window(j)
        mine = lane < NA_HEAD_DIM if a == 0 else lane >= NA_HEAD_DIM
        qa = jnp.where(mine, qn[j * sub_q:(j + 1) * sub_q, :], 0.0).astype(BF16)
        s = lax.dot_general(qa, kn_s[win, :], (((1,), (1,)), ((), ())), preferred_element_type=F32)
        return s + bias_ref[a, ty]

    outs = {}
    s = scores(*units[0])
    for u, (j, a) in enumerate(units):
        nxt = scores(*units[u + 1]) if u + 1 < len(units) else None
        m = jnp.max(s, axis=-1, keepdims=True)
        p = jnp.exp2(s - m)
        l = jnp.sum(p, axis=-1, keepdims=True)
        outs[j, a] = jnp.dot(p.astype(BF16), v_s[window(j)[0], :], preferred_element_type=F32) / l
        s = nxt
    for j in range(subs):
        merged = jnp.where(lane < NA_HEAD_DIM, outs[j, 0], outs[j, 1])
        o_ref[0, j * sub_q:(j + 1) * sub_q, :] = merged.astype(o_ref.dtype)


def _na_attn(qkv, qn, kn, bias_tables, b, t, subs=4):
    rows = t // GRID_W
    nhp = NA_WIDTH // LANES
    sub_q = NA_QROWS * GRID_W
    tq = subs * sub_q
    nrb = t // tq
    tkw = NA_KROWS * GRID_W
    return pl.pallas_call(
        functools.partial(_na_kernel, rows=rows, chunk=min(512, t), subs=subs, sub_q=sub_q),
        grid=(nhp, b, nrb),
        in_specs=[pl.BlockSpec((1, tq, LANES), lambda hp, i, r: (i, r, hp)),
                  pl.BlockSpec((1, t, LANES), lambda hp, i, r: (i, 0, nhp + hp)),
                  pl.BlockSpec((1, t, LANES), lambda hp, i, r: (i, 0, 2 * nhp + hp)),
                  pl.BlockSpec((1, LANES), lambda hp, i, r: (0, 0)),
                  pl.BlockSpec((1, LANES), lambda hp, i, r: (0, 0)),
                  pl.BlockSpec((2, 3, sub_q, tkw), lambda hp, i, r: (hp, 0, 0, 0))],
        out_specs=pl.BlockSpec((1, tq, LANES), lambda hp, i, r: (i, r, hp)),
        out_shape=jax.ShapeDtypeStruct((b, t, NA_WIDTH), BF16),
        scratch_shapes=[pltpu.VMEM((t, LANES), BF16), pltpu.VMEM((t, LANES), BF16)],
        name="na_attn",
        compiler_params=_cparams("parallel", "parallel", "arbitrary"),
    )(qkv, qkv, qkv, qn, kn, bias_tables)


def _merge_kernel(ap_ref, am_ref, an_ref, wp_ref, wm_ref, wn_ref, g0_ref, g1_ref, g2_ref, o_ref):
    acc = g0_ref[...].astype(F32) * jnp.dot(ap_ref[...], wp_ref[...], preferred_element_type=F32)
    acc = acc + g1_ref[...].astype(F32) * jnp.dot(am_ref[...], wm_ref[...], preferred_element_type=F32)
    acc = acc + g2_ref[...].astype(F32) * jnp.dot(an_ref[...], wn_ref[...], preferred_element_type=F32)
    o_ref[...] = acc.astype(o_ref.dtype)


def _merge(a_pool, a_mla, a_na, w_pool, w_mla, w_na, gates, tm=1024, tn=512):
    m = a_pool.shape[0]
    d = w_pool.shape[1]
    nj = d // tn
    a_spec = lambda a: pl.BlockSpec((tm, a.shape[1]), lambda j, i: (i, 0))
    w_spec = lambda w: pl.BlockSpec((w.shape[0], tn), lambda j, i: (0, j))
    g_spec = lambda br: pl.BlockSpec((tm, tn), lambda j, i: (i, br * nj + j))
    return pl.pallas_call(
        _merge_kernel,
        grid=(nj, m // tm),
        in_specs=[a_spec(a_pool), a_spec(a_mla), a_spec(a_na), w_spec(w_pool), w_spec(w_mla), w_spec(w_na),
                  g_spec(0), g_spec(1), g_spec(2)],
        out_specs=pl.BlockSpec((tm, tn), lambda j, i: (i, j)),
        out_shape=jax.ShapeDtypeStruct((m, d), BF16),
        name="branch_merge",
        compiler_params=_cparams("parallel", "parallel"),
    )(a_pool, a_mla, a_na, w_pool, w_mla, w_na, gates, gates, gates)


def _route_kernel(aff_ref, pos_ref, idx_ref, gate_ref, off_ref, *, cap, idx_chunk):
    a = aff_ref[0]
    r = a.shape[0]
    n = r * LANES
    bits = lax.bitcast_convert_type(a, I32)
    tok = lax.broadcasted_iota(I32, (r, LANES), 0) * LANES + lax.broadcasted_iota(I32, (r, LANES), 1)

    def count(mask):
        return jnp.sum(jnp.sum(mask.astype(F32), axis=1, keepdims=True), axis=0, keepdims=True)

    thr = jnp.zeros((1, 1), I32)
    for bit in range(30, -1, -1):
        cand = thr | (1 << bit)
        thr = jnp.where(count(bits >= cand) >= cap, cand, thr)
    above = bits > thr
    tied = bits == thr
    need = cap - count(above)
    last = jnp.zeros((1, 1), I32)
    for bit in range(int(math.log2(n)) - 1, -1, -1):
        cand = last | (1 << bit)
        last = jnp.where(count(tied & (tok < cand)) < need, cand, last)
    sel = above | (tied & (tok <= last))
    m = sel.astype(BF16)

    li = lax.broadcasted_iota(I32, (LANES, LANES), 0)
    lj = lax.broadcasted_iota(I32, (LANES, LANES), 1)
    upper = (li <= lj).astype(BF16)
    c1 = jnp.dot(m, upper, preferred_element_type=F32)
    rowtot = jnp.broadcast_to(c1[:, LANES - 1:LANES], (r, LANES))
    ri = lax.broadcasted_iota(I32, (r, r), 0)
    rj = lax.broadcasted_iota(I32, (r, r), 1)
    rowoff = jnp.dot((rj < ri).astype(BF16), rowtot.astype(BF16), preferred_element_type=F32)
    pos_ref[0] = jnp.where(sel, c1 - 1.0 + rowoff, -1.0).astype(I32)
    off_ref[0] = rowoff.astype(I32)

    tot_l = lax.dot_general(jnp.ones((8, LANES), BF16), m, (((1,), (1,)), ((), ())),
                            preferred_element_type=F32)
    cum_l = jnp.dot(tot_l.astype(BF16), (ri <= rj).astype(BF16), preferred_element_type=F32)[0:1, :]
    excl_l = cum_l - tot_l[0:1, :]
    lane_r = lax.broadcasted_iota(I32, (idx_chunk, r), 1)
    lane_t = lax.broadcasted_iota(I32, (idx_chunk, LANES), 1)
    for c in range(cap // idx_chunk):
        p = (c * idx_chunk + lax.broadcasted_iota(I32, (idx_chunk, 1), 0)).astype(F32)
        row = jnp.sum((cum_l <= p).astype(F32), axis=1, keepdims=True)
        onehot = lane_r == row.astype(I32)
        off = jnp.sum(jnp.where(onehot, excl_l, 0.0), axis=1, keepdims=True)
        mrow = jnp.dot(onehot.astype(BF16), m, preferred_element_type=F32)
        crow = jnp.dot(mrow.astype(BF16), upper, preferred_element_type=F32)
        lane = jnp.sum((crow <= p - off).astype(F32), axis=1, keepdims=True)
        arow = jnp.dot(onehot.astype(F32), a, precision=lax.Precision.HIGHEST, preferred_element_type=F32)
        sl = slice(c * idx_chunk, (c + 1) * idx_chunk)
        idx_ref[0, sl, :] = (row * LANES + lane).astype(I32)
        gate = jnp.sum(jnp.where(lane_t == lane.astype(I32), arow, 0.0), axis=1, keepdims=True)
        gate_ref[0, sl, :] = jnp.broadcast_to(gate, (idx_chunk, LANES))


def _route(aff3, cap):
    e, r, _ = aff3.shape
    idx_chunk = min(512, cap)
    blk = pl.BlockSpec((1, r, LANES), lambda i: (i, 0, 0))
    slot_blk = pl.BlockSpec((1, cap, 1), lambda i: (i, 0, 0))
    return pl.pallas_call(
        functools.partial(_route_kernel, cap=cap, idx_chunk=idx_chunk),
        grid=(e,),
        in_specs=[blk],
        out_specs=[blk, slot_blk, pl.BlockSpec((1, cap, LANES), lambda i: (i, 0, 0)), blk],
        out_shape=[jax.ShapeDtypeStruct((e, r, LANES), I32), jax.ShapeDtypeStruct((e, cap, 1), I32),
                   jax.ShapeDtypeStruct((e, cap, LANES), F32), jax.ShapeDtypeStruct((e, r, LANES), I32)],
        name="route",
        compiler_params=_cparams("parallel"),
    )(aff3)


FFN_BUFFERS = 3


def _ffn_kernel(idx_ref, h_hbm, gate_ref, wg_ref, wu_ref, wd_ref, o_ref, xbuf, sem, *, tm):
    nb = pl.num_programs(1)
    step = pl.program_id(0) * nb + pl.program_id(1)
    last = pl.num_programs(0) * nb - 1
    ahead = FFN_BUFFERS - 1

    def row_copy(block, slot_, r):
        tok = idx_ref[block * tm + r]
        return pltpu.make_async_copy(h_hbm.at[pl.ds(tok, 1), :], xbuf.at[slot_, pl.ds(r, 1), :], sem.at[slot_])

    def wait_block(slot_):
        pltpu.make_async_copy(h_hbm.at[pl.ds(0, tm), :], xbuf.at[slot_], sem.at[slot_]).wait()

    @pl.when(step == 0)
    def _():
        for blk in range(ahead):
            def issue(r, c, blk=blk):
                row_copy(jnp.minimum(blk, last), blk, r).start()
                return c
            lax.fori_loop(0, tm, issue, 0, unroll=8)

    slot = step % FFN_BUFFERS
    wait_block(slot)
    xe = xbuf[slot].astype(BF16)
    nxt = jnp.minimum(step + ahead, last)
    nxt_slot = (step + ahead) % FFN_BUFFERS
    for r in range(tm):
        row_copy(nxt, nxt_slot, r).start()
    gate = jnp.dot(xe, wg_ref[0], preferred_element_type=F32)
    up = jnp.dot(xe, wu_ref[0], preferred_element_type=F32)
    hid = (gate * jax.nn.sigmoid(gate) * up).astype(BF16)
    ye = jnp.dot(hid, wd_ref[0], preferred_element_type=F32)
    slot_gate = jnp.concatenate([gate_ref[0]] * (ye.shape[1] // LANES), axis=1)
    o_ref[0] = (ye * slot_gate).astype(o_ref.dtype)

    @pl.when(step == last)
    def _():
        for k in range(1, FFN_BUFFERS):
            wait_block((step + k) % FFN_BUFFERS)


def _expert_ffn(idx, slot_gate, h, w_gate, w_up, w_down, tm=256):
    e, cap, _ = slot_gate.shape
    d = h.shape[1]
    ff = w_gate.shape[2]
    nb = cap // tm
    grid_spec = pltpu.PrefetchScalarGridSpec(
        num_scalar_prefetch=1,
        grid=(e, nb),
        in_specs=[pl.BlockSpec(memory_space=pl.ANY),
                  pl.BlockSpec((1, tm, LANES), lambda i, j, s: (i, j, 0)),
                  pl.BlockSpec((1, d, ff), lambda i, j, s: (i, 0, 0)),
                  pl.BlockSpec((1, d, ff), lambda i, j, s: (i, 0, 0)),
                  pl.BlockSpec((1, ff, d), lambda i, j, s: (i, 0, 0))],
        out_specs=pl.BlockSpec((1, tm, d), lambda i, j, s: (i, j, 0)),
        scratch_shapes=[pltpu.VMEM((FFN_BUFFERS, tm, d), F32), pltpu.SemaphoreType.DMA((FFN_BUFFERS,))],
    )
    return pl.pallas_call(
        functools.partial(_ffn_kernel, tm=tm),
        grid_spec=grid_spec,
        out_shape=jax.ShapeDtypeStruct((e, cap, d), BF16),
        name="expert_ffn",
        compiler_params=_cparams("arbitrary", "arbitrary"),
    )(idx, h, slot_gate, w_gate, w_up, w_down)


COMBINE_WINDOW = LANES // 2
SLOT_ALIGN = 16


def _combine_kernel(starts_ref, pos_ref, x_ref, ye_hbm, o_ref, ycat, yextra, sem, sem_extra, *, tm, cap, nt):
    i = pl.program_id(0)
    n_exp = pos_ref.shape[1]
    w = COMBINE_WINDOW
    slot = i % 2

    def base_of(e, tile):
        start = starts_ref[e * (nt + 1) + tile]
        return jnp.minimum((start // SLOT_ALIGN) * SLOT_ALIGN, cap - w)

    def window_copy(e, first_row, dst, dst_sem):
        src = ye_hbm.at[e, pl.ds(pl.multiple_of(first_row, SLOT_ALIGN), w), :]
        return pltpu.make_async_copy(src, dst.at[pl.ds(e * w, w), :], dst_sem)

    def prefetch(tile, slot_):
        for e in range(n_exp):
            window_copy(e, base_of(e, tile), ycat.at[slot_], sem.at[slot_]).start()

    @pl.when(i == 0)
    def _():
        prefetch(0, 0)

    @pl.when(i + 1 < nt)
    def _():
        prefetch(i + 1, 1 - slot)

    bases = [base_of(e, i) for e in range(n_exp)]
    lane = lax.broadcasted_iota(I32, (tm, LANES), 1)
    low_half = lane < w

    def onehot(first_rows, floors):
        cols = []
        for e in range(0, n_exp, 2):
            pos = jnp.where(low_half, pos_ref[:, e:e + 1], pos_ref[:, e + 1:e + 2])
            hit = lane == pos - jnp.where(low_half, first_rows[e], first_rows[e + 1] - w)
            if floors is not None:
                hit = hit & (pos >= jnp.where(low_half, floors[e], floors[e + 1]))
            cols.append(hit.astype(BF16))
        return jnp.concatenate(cols, axis=1)

    sel = onehot(bases, None)
    for e in range(n_exp):
        window_copy(e, bases[e], ycat.at[slot], sem.at[slot]).wait()
    o_ref[...] = x_ref[...] + jnp.dot(sel, ycat[slot], preferred_element_type=F32)

    npass = jnp.int32(0)
    for e in range(n_exp):
        end = starts_ref[e * (nt + 1) + i + 1]
        npass = jnp.maximum(npass, (end - bases[e] + w - 1) // w)

    def extra_pass(j, c):
        floors = [bases[e] + j * w for e in range(n_exp)]
        first_rows = [jnp.minimum(f, cap - w) for f in floors]
        for e in range(n_exp):
            window_copy(e, first_rows[e], yextra, sem_extra).start()
        for e in range(n_exp):
            window_copy(e, first_rows[e], yextra, sem_extra).wait()
        o_ref[...] += jnp.dot(onehot(first_rows, floors), yextra[...], preferred_element_type=F32)
        return c

    lax.fori_loop(1, npass, extra_pass, 0)


def _combine(starts, pos_t, x, ye, tm=256):
    n, d = x.shape
    e, cap, _ = ye.shape
    nt = n // tm
    grid_spec = pltpu.PrefetchScalarGridSpec(
        num_scalar_prefetch=1,
        grid=(nt,),
        in_specs=[pl.BlockSpec((tm, e), lambda i, s: (i, 0)),
                  pl.BlockSpec((tm, d), lambda i, s: (i, 0)),
                  pl.BlockSpec(memory_space=pl.ANY)],
        out_specs=pl.BlockSpec((tm, d), lambda i, s: (i, 0)),
        scratch_shapes=[pltpu.VMEM((2, e * COMBINE_WINDOW, d), BF16), pltpu.VMEM((e * COMBINE_WINDOW, d), BF16),
                        pltpu.SemaphoreType.DMA((2,)), pltpu.SemaphoreType.DMA(())],
    )
    return pl.pallas_call(
        functools.partial(_combine_kernel, tm=tm, cap=cap, nt=nt),
        grid_spec=grid_spec,
        out_shape=jax.ShapeDtypeStruct((n, d), F32),
        name="expert_combine",
        compiler_params=_cparams("arbitrary"),
    )(starts.reshape(-1), pos_t, x, ye)


def _expert_choice_ffn(x, g_ffn, w_router_t, w_gate, w_up, w_down, tm=256):
    n, _ = x.shape
    e = w_router_t.shape[0]
    cap = EC_CAPACITY * n // e
    h, aff = _norm_router(x, g_ffn, w_router_t)
    pos, idx, slot_gate, off = _route(aff.reshape(e, n // LANES, LANES), cap)
    rows_per_tile = tm // LANES
    starts = jnp.concatenate([off[:, ::rows_per_tile, 0], jnp.full((e, 1), cap, I32)], axis=1)
    ye = _expert_ffn(idx.reshape(e * cap), slot_gate, h, w_gate, w_up, w_down, tm=tm)
    return _combine(starts, pos.reshape(e, n).T, x, ye, tm=tm)


def _prepare_layer(l, t, g_mix, w_in, pool_w, pool_scale, mla_gq, mla_gkv, mla_w_uq, mla_w_ukv, mla_q_norm,
                   mla_k_norm, na_q_norm, na_k_norm, na_rel_bias, b_gate, w_br_pool, w_br_mla, w_br_na, w_out,
                   g_ffn, w_router, w_e_gate, w_e_up, w_e_down):
    d = w_in.shape[1]
    off_cq = POOL_WIDTH
    off_na = off_cq + 2 * MLA_RANK + MLA_ROPE
    off_gate = off_na + 3 * NA_WIDTH
    wl = w_in[l]
    p = {}
    p["g_mix"] = g_mix[l]
    p["w_pool_in"] = wl[:, :off_cq].astype(BF16)
    p["w_mla_in"] = jnp.pad(wl[:, off_cq:off_na], ((0, 0), (0, MLA_SEG - (off_na - off_cq)))).astype(BF16)
    p["w_na_in"] = wl[:, off_na:off_gate].astype(BF16)
    p["w_gate_in"] = wl[:, off_gate:].astype(BF16)
    p["b_gate"] = b_gate[l].reshape(N_BRANCH * d)
    p["pool_w"] = pool_w[l].astype(BF16)
    p["pool_scale"] = pool_scale[l]
    p["gq"] = mla_gq[l].reshape(1, MLA_RANK)
    p["gkv"] = mla_gkv[l].reshape(1, MLA_RANK)
    wq = mla_w_uq[l].reshape(MLA_RANK, MLA_HEADS, MLA_QK)
    wq_rope = jnp.pad(wq[:, :, MLA_NOPE:], ((0, 0), (0, 0), (0, LANES - MLA_ROPE)))
    p["wuq"] = jnp.concatenate([wq[:, :, :MLA_NOPE].reshape(MLA_RANK, -1), wq_rope.reshape(MLA_RANK, -1)],
                               axis=1).astype(BF16)
    wkv = mla_w_ukv[l].reshape(MLA_RANK, MLA_HEADS, MLA_NOPE + MLA_V)
    p["wuk"] = wkv[:, :, :MLA_NOPE].reshape(MLA_RANK, -1).astype(BF16)
    p["wvt"] = wkv[:, :, MLA_NOPE:].reshape(MLA_RANK, -1).T.astype(BF16)
    pad_norm = lambda g: jnp.pad(g, (0, 2 * LANES - MLA_QK)).reshape(1, 2 * LANES)
    p["qn"] = pad_norm(mla_q_norm[l] * (MLA_QK ** -0.5 * LOG2E))
    p["kn"] = pad_norm(mla_k_norm[l])
    p["na_qn"] = jnp.tile(na_q_norm[l] * (NA_HEAD_DIM ** -0.5 * LOG2E), 2).reshape(1, LANES)
    p["na_kn"] = jnp.tile(na_k_norm[l], 2).reshape(1, LANES)
    p["na_bias"] = _na_bias_tables(na_rel_bias[l], t // GRID_W)
    p["w_br_pool"] = w_br_pool[l].astype(BF16)
    p["w_br_mla"] = w_br_mla[l].astype(BF16)
    p["w_br_na"] = w_br_na[l].astype(BF16)
    p["w_out"] = w_out[l].astype(BF16)
    p["g_ffn"] = g_ffn[l]
    p["w_router_t"] = w_router[l].T
    p["w_e_gate"] = w_e_gate[l].astype(BF16)
    p["w_e_up"] = w_e_up[l].astype(BF16)
    p["w_e_down"] = w_e_down[l].astype(BF16)
    return p


def _rope_tables(t):
    half = MLA_ROPE // 2
    pos = jnp.arange(t, dtype=F32)
    inv = 1.0 / (ROPE_THETA ** (jnp.arange(0, MLA_ROPE, 2, dtype=F32) / MLA_ROPE))
    ang = pos[:, None] * inv[None, :]
    cos = jnp.cos(ang)
    sin = jnp.sin(ang)
    zeros = jnp.zeros((t, LANES - MLA_ROPE), F32)
    return jnp.concatenate([cos, cos, zeros], axis=1), jnp.concatenate([-sin, sin, zeros], axis=1)


def _layer(x, p, cos_t, sin_t, b, t):
    m, d = x.shape
    h = _rmsnorm(x, p["g_mix"])
    u = _matmul(h, p["w_pool_in"], F32, tn=POOL_WIDTH, name="in_proj_pool")
    mla_seg = _matmul(h, p["w_mla_in"], F32, tn=MLA_SEG, name="in_proj_mla")
    na_qkv = _matmul(h, p["w_na_in"], F32, tn=1024, name="in_proj_na")
    gates = _matmul(h, p["w_gate_in"], BF16, tn=1024, name="in_proj_gates", bias=p["b_gate"])
    a_pool = _pool_mixer(u.reshape(b, t, POOL_WIDTH), p["pool_w"], p["pool_scale"]).reshape(m, POOL_WIDTH)
    q, k, vt = _mla_prep(mla_seg, p["gq"], p["gkv"], p["wuq"], p["wuk"], p["wvt"], p["qn"], p["kn"], cos_t, sin_t,
                         b, t)
    a_mla = _mla_attn(q, k, vt).reshape(m, MLA_HEADS * MLA_V)
    a_na = _na_attn(na_qkv.reshape(b, t, 3 * NA_WIDTH), p["na_qn"], p["na_kn"], p["na_bias"], b, t).reshape(m, NA_WIDTH)
    merged = _merge(a_pool, a_mla, a_na, p["w_br_pool"], p["w_br_mla"], p["w_br_na"], gates)
    x = _matmul(merged, p["w_out"], F32, tn=1024, name="out_proj", residual=x)
    return _expert_choice_ffn(x, p["g_ffn"], p["w_router_t"], p["w_e_gate"], p["w_e_up"], p["w_e_down"])


def _trunk(x, layers, cos_t, sin_t):
    b, t, d = x.shape
    x = x.reshape(b * t, d)
    for p in layers:
        x = _layer(x, p, cos_t, sin_t, b, t)
    return x.reshape(b, t, d)


def kernel(x_prompt, x_sample, g_mix, w_in, pool_w, pool_scale, mla_gq, mla_gkv, mla_w_uq, mla_w_ukv, mla_q_norm,
           mla_k_norm, na_q_norm, na_k_norm, na_rel_bias, b_gate, w_br_pool, w_br_mla, w_br_na, w_out, g_ffn,
           w_router, w_e_gate, w_e_up, w_e_down):
    weights = (g_mix, w_in, pool_w, pool_scale, mla_gq, mla_gkv, mla_w_uq, mla_w_ukv, mla_q_norm, mla_k_norm,
               na_q_norm, na_k_norm, na_rel_bias, b_gate, w_br_pool, w_br_mla, w_br_na, w_out, g_ffn, w_router,
               w_e_gate, w_e_up, w_e_down)
    prepared = {}
    outs = []
    for x in (x_prompt, x_sample):
        t = x.shape[1]
        if t not in prepared:
            prepared[t] = ([_prepare_layer(l, t, *weights) for l in range(g_mix.shape[0])], _rope_tables(t))
        layers, (cos_t, sin_t) = prepared[t]
        outs.append(_trunk(x, layers, cos_t, sin_t))
    return tuple(outs)
```

```python
import functools
import math

import numpy as np
import jax
import jax.numpy as jnp
from jax import lax
from jax.experimental import pallas as pl
from jax.experimental.pallas import tpu as pltpu

F32 = jnp.float32
BF16 = jnp.bfloat16
I32 = jnp.int32

LANES = 128
VMEM_LIMIT = 56 * 1024 * 1024

GRID_W = 64
POOL_GROUPS = 4
POOL_GROUP_DIM = 256
POOL_WIDTH = POOL_GROUPS * POOL_GROUP_DIM
POOL_WINDOWS = (2, 4, 8, 16)
POOL_HALO = 8
MLA_HEADS = 16
MLA_RANK = 512
MLA_NOPE = 128
MLA_ROPE = 64
MLA_QK = MLA_NOPE + MLA_ROPE
MLA_QK_PAD = 256
MLA_V = 128
MLA_SEG = 2 * MLA_RANK + LANES
ROPE_THETA = 10000.0
NA_HEADS = 16
NA_HEAD_DIM = 64
NA_WIDTH = NA_HEADS * NA_HEAD_DIM
NA_KH = 8
NA_KW = 16
NA_QROWS = 4
NA_KROWS = NA_QROWS + NA_KH
N_BRANCH = 3
N_EXPERTS = 16
EC_CAPACITY = 2
EPS = 1e-6
NEG = -1e30
LOG2E = math.log2(math.e)


def _cparams(*sem):
    return pltpu.CompilerParams(dimension_semantics=sem, vmem_limit_bytes=VMEM_LIMIT)


def _rmsnorm_kernel(x_ref, g_ref, o_ref):
    x = x_ref[...]
    y = x * lax.rsqrt(jnp.mean(x * x, axis=-1, keepdims=True) + EPS) * g_ref[...]
    o_ref[...] = y.astype(o_ref.dtype)


def _rmsnorm(x, g, tm=512):
    m, d = x.shape
    return pl.pallas_call(
        _rmsnorm_kernel,
        grid=(m // tm,),
        in_specs=[pl.BlockSpec((tm, d), lambda i: (i, 0)), pl.BlockSpec((1, d), lambda i: (0, 0))],
        out_specs=pl.BlockSpec((tm, d), lambda i: (i, 0)),
        out_shape=jax.ShapeDtypeStruct((m, d), BF16),
        name="rmsnorm",
        compiler_params=_cparams("parallel"),
    )(x, g.reshape(1, d))


def _norm_router_kernel(x_ref, g_ref, wr_ref, h_ref, aff_ref):
    x = x_ref[...]
    y = x * lax.rsqrt(jnp.mean(x * x, axis=-1, keepdims=True) + EPS) * g_ref[...]
    h_ref[...] = y
    logits = lax.dot_general(wr_ref[...], y, (((1,), (1,)), ((), ())), precision=lax.Precision.HIGHEST,
                             preferred_element_type=F32)
    mx = jnp.max(logits, axis=0, keepdims=True)
    ex = jnp.exp(logits - mx)
    aff_ref[...] = ex / jnp.sum(ex, axis=0, keepdims=True)


def _norm_router(x, g, w_router_t, tm=512):
    m, d = x.shape
    e = w_router_t.shape[0]
    return pl.pallas_call(
        _norm_router_kernel,
        grid=(m // tm,),
        in_specs=[pl.BlockSpec((tm, d), lambda i: (i, 0)), pl.BlockSpec((1, d), lambda i: (0, 0)),
                  pl.BlockSpec((e, d), lambda i: (0, 0))],
        out_specs=[pl.BlockSpec((tm, d), lambda i: (i, 0)), pl.BlockSpec((e, tm), lambda i: (0, i))],
        out_shape=[jax.ShapeDtypeStruct((m, d), F32), jax.ShapeDtypeStruct((e, m), F32)],
        name="norm_router",
        compiler_params=_cparams("parallel"),
    )(x, g.reshape(1, d), w_router_t)


def _mm_kernel(a_ref, w_ref, *rest, epilogue):
    acc = jnp.dot(a_ref[...], w_ref[...], preferred_element_type=F32)
    if epilogue == "sigmoid_bias":
        b_ref, o_ref = rest
        acc = jax.nn.sigmoid(acc + b_ref[...])
    elif epilogue == "residual":
        r_ref, o_ref = rest
        acc = acc + r_ref[...]
    else:
        (o_ref,) = rest
    o_ref[...] = acc.astype(o_ref.dtype)


def _matmul(a, w, out_dtype, tn, name, tm=1024, bias=None, residual=None):
    m, k = a.shape
    n = w.shape[1]
    in_specs = [pl.BlockSpec((tm, k), lambda j, i: (i, 0)), pl.BlockSpec((k, tn), lambda j, i: (0, j))]
    args = [a, w]
    epilogue = "none"
    if bias is not None:
        epilogue = "sigmoid_bias"
        in_specs.append(pl.BlockSpec((1, tn), lambda j, i: (0, j)))
        args.append(bias.reshape(1, n))
    if residual is not None:
        epilogue = "residual"
        in_specs.append(pl.BlockSpec((tm, tn), lambda j, i: (i, j)))
        args.append(residual)
    return pl.pallas_call(
        functools.partial(_mm_kernel, epilogue=epilogue),
        grid=(n // tn, m // tm),
        in_specs=in_specs,
        out_specs=pl.BlockSpec((tm, tn), lambda j, i: (i, j)),
        out_shape=jax.ShapeDtypeStruct((m, n), out_dtype),
        name=name,
        compiler_params=_cparams("parallel", "parallel"),
    )(*args)


def _pool_kernel(u_ref, w_ref, s_ref, o_ref, pad_ref, *, seq, chunk):
    g = pl.program_id(1)
    zeros = jnp.zeros((POOL_HALO, POOL_GROUP_DIM), F32)
    pad_ref[0:POOL_HALO, :] = zeros
    pad_ref[POOL_HALO + seq:2 * POOL_HALO + seq, :] = zeros
    pad_ref[POOL_HALO:POOL_HALO + seq, :] = u_ref[0]

    def window(w):
        left = w // 2
        right = w - 1 - left
        for c in range(seq // chunk):
            base = POOL_HALO + c * chunk
            acc = pad_ref[base - left:base - left + chunk, :]
            for d in range(-left + 1, right + 1):
                acc = acc + pad_ref[base + d:base + d + chunk, :]
            t = c * chunk + lax.broadcasted_iota(I32, (chunk, 1), 0)
            cnt = (jnp.minimum(t + right, seq - 1) + 1 - jnp.maximum(t - left, 0)).astype(F32)
            pooled = acc / cnt - pad_ref[base:base + chunk, :]
            mixed = jnp.dot(pooled.astype(BF16), w_ref[0], preferred_element_type=F32) * s_ref[...]
            o_ref[0, c * chunk:(c + 1) * chunk, :] = mixed.astype(o_ref.dtype)

    for k, w in enumerate(POOL_WINDOWS):
        pl.when(g == k)(functools.partial(window, w))


def _pool_mixer(u, pool_w, pool_scale):
    b, t, _ = u.shape
    chunk = min(512, t)
    return pl.pallas_call(
        functools.partial(_pool_kernel, seq=t, chunk=chunk),
        grid=(b, POOL_GROUPS),
        in_specs=[pl.BlockSpec((1, t, POOL_GROUP_DIM), lambda i, g: (i, 0, g)),
                  pl.BlockSpec((1, POOL_GROUP_DIM, POOL_GROUP_DIM), lambda i, g: (g, 0, 0)),
                  pl.BlockSpec((1, POOL_GROUP_DIM), lambda i, g: (0, g))],
        out_specs=pl.BlockSpec((1, t, POOL_GROUP_DIM), lambda i, g: (i, 0, g)),
        out_shape=jax.ShapeDtypeStruct((b, t, POOL_WIDTH), BF16),
        scratch_shapes=[pltpu.VMEM((t + 2 * POOL_HALO, POOL_GROUP_DIM), F32)],
        name="pool_mixer",
        compiler_params=_cparams("parallel", "parallel"),
    )(u, pool_w, pool_scale.reshape(1, POOL_WIDTH))


def _mla_prep_kernel(seg_ref, gq_ref, gkv_ref, wuq_ref, wuk_ref, wvt_ref, qn_ref, kn_ref, cos_ref, sin_ref,
                     q_ref, k_ref, vt_ref):
    def rms(x):
        return x * lax.rsqrt(jnp.mean(x * x, axis=-1, keepdims=True) + EPS)

    def rope(x):
        lane = lax.broadcasted_iota(I32, x.shape, 1)
        partner = jnp.where(lane < MLA_ROPE // 2, pltpu.roll(x, LANES - MLA_ROPE // 2, 1),
                            pltpu.roll(x, MLA_ROPE // 2, 1))
        return x * cos_ref[...] + partner * sin_ref[...]

    cq = (rms(seg_ref[:, 0:MLA_RANK]) * gq_ref[...]).astype(BF16)
    ckv = (rms(seg_ref[:, MLA_RANK:2 * MLA_RANK]) * gkv_ref[...]).astype(BF16)
    kr = seg_ref[:, 2 * MLA_RANK:2 * MLA_RANK + LANES]
    q = jnp.dot(cq, wuq_ref[...], preferred_element_type=F32)
    kn = jnp.dot(ckv, wuk_ref[...], preferred_element_type=F32)
    vt = lax.dot_general(wvt_ref[...], ckv, (((1,), (1,)), ((), ())), preferred_element_type=F32)
    hn = MLA_HEADS * MLA_NOPE
    kr_sq = kr * kr
    kr_rot = rope(kr * kn_ref[:, LANES:2 * LANES])
    inv_d = 1.0 / MLA_QK
    for h in range(MLA_HEADS):
        qa = q[:, h * LANES:(h + 1) * LANES]
        qb = q[:, hn + h * LANES:hn + (h + 1) * LANES]
        rs = lax.rsqrt(jnp.sum(qa * qa + qb * qb, axis=-1, keepdims=True) * inv_d + EPS)
        q_ref[0, h, :, 0:LANES] = (qa * rs * qn_ref[:, 0:LANES]).astype(BF16)
        q_ref[0, h, :, LANES:2 * LANES] = rope(qb * rs * qn_ref[:, LANES:2 * LANES]).astype(BF16)
        ka = kn[:, h * LANES:(h + 1) * LANES]
        rs = lax.rsqrt(jnp.sum(ka * ka + kr_sq, axis=-1, keepdims=True) * inv_d + EPS)
        k_ref[0, h, :, 0:LANES] = (ka * rs * kn_ref[:, 0:LANES]).astype(BF16)
        k_ref[0, h, :, LANES:2 * LANES] = (kr_rot * rs).astype(BF16)
        vt_ref[0, h, :, :] = vt[h * MLA_V:(h + 1) * MLA_V, :].astype(BF16)


def _mla_prep(seg, gq, gkv, wuq, wuk, wvt, qn, kn, cos_t, sin_t, b, t, tm=256):
    m = b * t
    tpb = t // tm
    const = lambda i: (0, 0)
    head_map = lambda i: (i // tpb, 0, i % tpb, 0)
    return pl.pallas_call(
        _mla_prep_kernel,
        grid=(m // tm,),
        in_specs=[pl.BlockSpec((tm, MLA_SEG), lambda i: (i, 0)),
                  pl.BlockSpec((1, MLA_RANK), const), pl.BlockSpec((1, MLA_RANK), const),
                  pl.BlockSpec(wuq.shape, const), pl.BlockSpec(wuk.shape, const), pl.BlockSpec(wvt.shape, const),
                  pl.BlockSpec((1, 2 * LANES), const), pl.BlockSpec((1, 2 * LANES), const),
                  pl.BlockSpec((tm, LANES), lambda i: (i % tpb, 0)), pl.BlockSpec((tm, LANES), lambda i: (i % tpb, 0))],
        out_specs=[pl.BlockSpec((1, MLA_HEADS, tm, MLA_QK_PAD), head_map),
                   pl.BlockSpec((1, MLA_HEADS, tm, MLA_QK_PAD), head_map),
                   pl.BlockSpec((1, MLA_HEADS, MLA_V, tm), lambda i: (i // tpb, 0, 0, i % tpb))],
        out_shape=[jax.ShapeDtypeStruct((b, MLA_HEADS, t, MLA_QK_PAD), BF16),
                   jax.ShapeDtypeStruct((b, MLA_HEADS, t, MLA_QK_PAD), BF16),
                   jax.ShapeDtypeStruct((b, MLA_HEADS, MLA_V, t), BF16)],
        name="mla_prep",
        compiler_params=_cparams("parallel"),
    )(seg, gq, gkv, wuq, wuk, wvt, qn, kn, cos_t, sin_t)


def _mla_attn_kernel(q_ref, k_ref, vt_ref, o_ref, *, sub, kc):
    n_sub = q_ref.shape[2] // sub
    n_kc = k_ref.shape[2] // kc

    def score_chunk(r, c):
        q = q_ref[0, 0, r * sub:(r + 1) * sub, :]
        k = k_ref[0, 0, c * kc:(c + 1) * kc, :]
        return lax.dot_general(k, q, (((1,), (1,)), ((), ())), preferred_element_type=F32)

    st = [score_chunk(0, c) for c in range(n_kc)]
    for r in range(n_sub):
        m = jnp.max(functools.reduce(jnp.maximum, st), axis=0, keepdims=True)
        nxt, l, ot = [], None, None
        for c in range(n_kc):
            if r + 1 < n_sub:
                nxt.append(score_chunk(r + 1, c))
            pt = jnp.exp2(st[c] - m)
            lc = jnp.sum(pt, axis=0, keepdims=True)
            oc = jnp.dot(vt_ref[0, 0, :, c * kc:(c + 1) * kc], pt.astype(BF16), preferred_element_type=F32)
            l = lc if l is None else l + lc
            ot = oc if ot is None else ot + oc
        o_ref[0, r * sub:(r + 1) * sub, :] = (ot / l).T.astype(o_ref.dtype)
        st = nxt


def _mla_attn(q, k, vt, tq=4096, sub=256, kc=2048):
    b, h, t, _ = q.shape
    tq, kc = min(tq, t), min(kc, t)
    return pl.pallas_call(
        functools.partial(_mla_attn_kernel, sub=sub, kc=kc),
        grid=(b, h, t // tq),
        in_specs=[pl.BlockSpec((1, 1, tq, MLA_QK_PAD), lambda i, j, n: (i, j, n, 0)),
                  pl.BlockSpec((1, 1, t, MLA_QK_PAD), lambda i, j, n: (i, j, 0, 0)),
                  pl.BlockSpec((1, 1, MLA_V, t), lambda i, j, n: (i, j, 0, 0))],
        out_specs=pl.BlockSpec((1, tq, MLA_V), lambda i, j, n: (i, n, j)),
        out_shape=jax.ShapeDtypeStruct((b, t, h * MLA_V), BF16),
        name="mla_attn",
        compiler_params=_cparams("parallel", "parallel", "parallel"),
    )(q, k, vt)


def _na_bias_tables(rel_bias, rows):
    kw = NA_KW
    exact = lax.Precision.HIGHEST
    qc = np.arange(GRID_W)[:, None]
    kc = np.arange(GRID_W)[None, :]
    sc = np.clip(qc - kw // 2, 0, GRID_W - kw)
    valid_c = (kc >= sc) & (kc < sc + kw)
    dx = np.clip(kc - qc + kw - 1, 0, 2 * kw - 2)
    onehot_x = (dx[:, :, None] == np.arange(2 * kw - 1)) & valid_c[:, :, None]
    by_col = jnp.einsum("hyx,qkx->hyqk", rel_bias, jnp.asarray(onehot_x, F32), precision=exact)
    qr = np.arange(NA_QROWS)[:, None]
    kr = np.arange(NA_KROWS)[None, :]
    tables = []
    for r0 in (0, NA_QROWS, rows - NA_QROWS):
        ks = int(np.clip(r0 - NA_KH // 2, 0, rows - NA_KROWS))
        qrow = r0 + qr
        krow = ks + kr
        sr = np.clip(qrow - NA_KH // 2, 0, rows - NA_KH)
        valid_r = (krow >= sr) & (krow < sr + NA_KH)
        dy = np.clip(krow - qrow + NA_KH - 1, 0, 2 * NA_KH - 2)
        onehot_y = (dy[:, :, None] == np.arange(2 * NA_KH - 1)) & valid_r[:, :, None]
        tbl = jnp.einsum("hyqk,rsy->hrqsk", by_col, jnp.asarray(onehot_y, F32), precision=exact)
        valid = valid_r[:, None, :, None] & valid_c[None, :, None, :]
        tbl = jnp.where(valid[None], tbl * LOG2E, NEG)
        tables.append(tbl.reshape(-1, NA_QROWS * GRID_W, NA_KROWS * GRID_W))
    return jnp.stack(tables, axis=1)


def _na_kernel(q_ref, k_ref, v_ref, qn_ref, kn_ref, bias_ref, o_ref, kn_s, v_s, *, rows, chunk, subs, sub_q):
    rb = pl.program_id(2)
    seq = rows * GRID_W
    inv_d = 1.0 / NA_HEAD_DIM

    def head_norm(x, g):
        lane = lax.broadcasted_iota(I32, x.shape, 1)
        lo = lane < NA_HEAD_DIM
        sq = x * x
        s0 = jnp.sum(jnp.where(lo, sq, 0.0), axis=-1, keepdims=True)
        s1 = jnp.sum(jnp.where(lo, 0.0, sq), axis=-1, keepdims=True)
        rs = lax.rsqrt(jnp.where(lo, s0, s1) * inv_d + EPS)
        return x * rs * g

    @pl.when(rb == 0)
    def _():
        for c in range(seq // chunk):
            sl = slice(c * chunk, (c + 1) * chunk)
            kn_s[sl, :] = head_norm(k_ref[0, sl, :], kn_ref[...]).astype(BF16)
            v_s[sl, :] = v_ref[0, sl, :].astype(BF16)

    qn = head_norm(q_ref[0], qn_ref[...])
    lane = lax.broadcasted_iota(I32, (sub_q, LANES), 1)
    n_blocks = rows // NA_QROWS
    units = [(j, a) for j in range(subs) for a in range(2)]

    def window(j):
        blk = rb * subs + j
        ks = jnp.clip(blk * NA_QROWS - NA_KH // 2, 0, rows - NA_KROWS)
        ty = jnp.where(blk == 0, 0, jnp.where(blk == n_blocks - 1, 2, 1))
        return pl.ds(pl.multiple_of(ks * GRID_W, GRID_W), NA_KROWS * GRID_W), ty

    def scores(j, a):
        win, ty = window(j)
        mine = lane < NA_HEAD_DIM if a == 0 else lane >= NA_HEAD_DIM
        qa = jnp.where(mine, qn[j * sub_q:(j + 1) * sub_q, :], 0.0).astype(BF16)
        s = lax.dot_general(qa, kn_s[win, :], (((1,), (1,)), ((), ())), preferred_element_type=F32)
        return s + bias_ref[a, ty]

    outs = {}
    s = scores(*units[0])
    for u, (j, a) in enumerate(units):
        nxt = scores(*units[u + 1]) if u + 1 < len(units) else None
        m = jnp.max(s, axis=-1, keepdims=True)
        p = jnp.exp2(s - m)
        l = jnp.sum(p, axis=-1, keepdims=True)
        outs[j, a] = jnp.dot(p.astype(BF16), v_s[window(j)[0], :], preferred_element_type=F32) / l
        s = nxt
    for j in range(subs):
        merged = jnp.where(lane < NA_HEAD_DIM, outs[j, 0], outs[j, 1])
        o_ref[0, j * sub_q:(j + 1) * sub_q, :] = merged.astype(o_ref.dtype)


def _na_attn(qkv, qn, kn, bias_tables, b, t, subs=4):
    rows = t // GRID_W
    nhp = NA_WIDTH // LANES
    sub_q = NA_QROWS * GRID_W
    tq = subs * sub_q
    nrb = t // tq
    tkw = NA_KROWS * GRID_W
    return pl.pallas_call(
        functools.partial(_na_kernel, rows=rows, chunk=min(512, t), subs=subs, sub_q=sub_q),
        grid=(nhp, b, nrb),
        in_specs=[pl.BlockSpec((1, tq, LANES), lambda hp, i, r: (i, r, hp)),
                  pl.BlockSpec((1, t, LANES), lambda hp, i, r: (i, 0, nhp + hp)),
                  pl.BlockSpec((1, t, LANES), lambda hp, i, r: (i, 0, 2 * nhp + hp)),
                  pl.BlockSpec((1, LANES), lambda hp, i, r: (0, 0)),
                  pl.BlockSpec((1, LANES), lambda hp, i, r: (0, 0)),
                  pl.BlockSpec((2, 3, sub_q, tkw), lambda hp, i, r: (hp, 0, 0, 0))],
        out_specs=pl.BlockSpec((1, tq, LANES), lambda hp, i, r: (i, r, hp)),
        out_shape=jax.ShapeDtypeStruct((b, t, NA_WIDTH), BF16),
        scratch_shapes=[pltpu.VMEM((t, LANES), BF16), pltpu.VMEM((t, LANES), BF16)],
        name="na_attn",
        compiler_params=_cparams("parallel", "parallel", "arbitrary"),
    )(qkv, qkv, qkv, qn, kn, bias_tables)


def _merge_kernel(ap_ref, am_ref, an_ref, wp_ref, wm_ref, wn_ref, g0_ref, g1_ref, g2_ref, o_ref):
    acc = g0_ref[...].astype(F32) * jnp.dot(ap_ref[...], wp_ref[...], preferred_element_type=F32)
    acc = acc + g1_ref[...].astype(F32) * jnp.dot(am_ref[...], wm_ref[...], preferred_element_type=F32)
    acc = acc + g2_ref[...].astype(F32) * jnp.dot(an_ref[...], wn_ref[...], preferred_element_type=F32)
    o_ref[...] = acc.astype(o_ref.dtype)


def _merge(a_pool, a_mla, a_na, w_pool, w_mla, w_na, gates, tm=1024, tn=512):
    m = a_pool.shape[0]
    d = w_pool.shape[1]
    nj = d // tn
    a_spec = lambda a: pl.BlockSpec((tm, a.shape[1]), lambda j, i: (i, 0))
    w_spec = lambda w: pl.BlockSpec((w.shape[0], tn), lambda j, i: (0, j))
    g_spec = lambda br: pl.BlockSpec((tm, tn), lambda j, i: (i, br * nj + j))
    return pl.pallas_call(
        _merge_kernel,
        grid=(nj, m // tm),
        in_specs=[a_spec(a_pool), a_spec(a_mla), a_spec(a_na), w_spec(w_pool), w_spec(w_mla), w_spec(w_na),
                  g_spec(0), g_spec(1), g_spec(2)],
        out_specs=pl.BlockSpec((tm, tn), lambda j, i: (i, j)),
        out_shape=jax.ShapeDtypeStruct((m, d), BF16),
        name="branch_merge",
        compiler_params=_cparams("parallel", "parallel"),
    )(a_pool, a_mla, a_na, w_pool, w_mla, w_na, gates, gates, gates)


def _route_kernel(aff_ref, pos_ref, idx_ref, gate_ref, off_ref, *, cap, idx_chunk):
    a = aff_ref[0]
    r = a.shape[0]
    n = r * LANES
    bits = lax.bitcast_convert_type(a, I32)
    tok = lax.broadcasted_iota(I32, (r, LANES), 0) * LANES + lax.broadcasted_iota(I32, (r, LANES), 1)

    def count(mask):
        return jnp.sum(jnp.sum(mask.astype(F32), axis=1, keepdims=True), axis=0, keepdims=True)

    thr = jnp.zeros((1, 1), I32)
    for bit in range(30, -1, -1):
        cand = thr | (1 << bit)
        thr = jnp.where(count(bits >= cand) >= cap, cand, thr)
    above = bits > thr
    tied = bits == thr
    need = cap - count(above)
    last = jnp.zeros((1, 1), I32)
    for bit in range(int(math.log2(n)) - 1, -1, -1):
        cand = last | (1 << bit)
        last = jnp.where(count(tied & (tok < cand)) < need, cand, last)
    sel = above | (tied & (tok <= last))
    m = sel.astype(BF16)

    li = lax.broadcasted_iota(I32, (LANES, LANES), 0)
    lj = lax.broadcasted_iota(I32, (LANES, LANES), 1)
    upper = (li <= lj).astype(BF16)
    c1 = jnp.dot(m, upper, preferred_element_type=F32)
    rowtot = jnp.broadcast_to(c1[:, LANES - 1:LANES], (r, LANES))
    ri = lax.broadcasted_iota(I32, (r, r), 0)
    rj = lax.broadcasted_iota(I32, (r, r), 1)
    rowoff = jnp.dot((rj < ri).astype(BF16), rowtot.astype(BF16), preferred_element_type=F32)
    pos_ref[0] = jnp.where(sel, c1 - 1.0 + rowoff, -1.0).astype(I32)
    off_ref[0] = rowoff.astype(I32)

    tot_l = lax.dot_general(jnp.ones((8, LANES), BF16), m, (((1,), (1,)), ((), ())),
                            preferred_element_type=F32)
    cum_l = jnp.dot(tot_l.astype(BF16), (ri <= rj).astype(BF16), preferred_element_type=F32)[0:1, :]
    excl_l = cum_l - tot_l[0:1, :]
    lane_r = lax.broadcasted_iota(I32, (idx_chunk, r), 1)
    lane_t = lax.broadcasted_iota(I32, (idx_chunk, LANES), 1)
    for c in range(cap // idx_chunk):
        p = (c * idx_chunk + lax.broadcasted_iota(I32, (idx_chunk, 1), 0)).astype(F32)
        row = jnp.sum((cum_l <= p).astype(F32), axis=1, keepdims=True)
        onehot = lane_r == row.astype(I32)
        off = jnp.sum(jnp.where(onehot, excl_l, 0.0), axis=1, keepdims=True)
        mrow = jnp.dot(onehot.astype(BF16), m, preferred_element_type=F32)
        crow = jnp.dot(mrow.astype(BF16), upper, preferred_element_type=F32)
        lane = jnp.sum((crow <= p - off).astype(F32), axis=1, keepdims=True)
        arow = jnp.dot(onehot.astype(F32), a, precision=lax.Precision.HIGHEST, preferred_element_type=F32)
        sl = slice(c * idx_chunk, (c + 1) * idx_chunk)
        idx_ref[0, sl, :] = (row * LANES + lane).astype(I32)
        gate = jnp.sum(jnp.where(lane_t == lane.astype(I32), arow, 0.0), axis=1, keepdims=True)
        gate_ref[0, sl, :] = jnp.broadcast_to(gate, (idx_chunk, LANES))


def _route(aff3, cap):
    e, r, _ = aff3.shape
    idx_chunk = min(512, cap)
    blk = pl.BlockSpec((1, r, LANES), lambda i: (i, 0, 0))
    slot_blk = pl.BlockSpec((1, cap, 1), lambda i: (i, 0, 0))
    return pl.pallas_call(
        functools.partial(_route_kernel, cap=cap, idx_chunk=idx_chunk),
        grid=(e,),
        in_specs=[blk],
        out_specs=[blk, slot_blk, pl.BlockSpec((1, cap, LANES), lambda i: (i, 0, 0)), blk],
        out_shape=[jax.ShapeDtypeStruct((e, r, LANES), I32), jax.ShapeDtypeStruct((e, cap, 1), I32),
                   jax.ShapeDtypeStruct((e, cap, LANES), F32), jax.ShapeDtypeStruct((e, r, LANES), I32)],
        name="route",
        compiler_params=_cparams("parallel"),
    )(aff3)


FFN_BUFFERS = 3


def _ffn_kernel(idx_ref, h_hbm, gate_ref, wg_ref, wu_ref, wd_ref, o_ref, xbuf, sem, *, tm):
    nb = pl.num_programs(1)
    step = pl.program_id(0) * nb + pl.program_id(1)
    last = pl.num_programs(0) * nb - 1
    ahead = FFN_BUFFERS - 1

    def row_copy(block, slot_, r):
        tok = idx_ref[block * tm + r]
        return pltpu.make_async_copy(h_hbm.at[pl.ds(tok, 1), :], xbuf.at[slot_, pl.ds(r, 1), :], sem.at[slot_])

    def wait_block(slot_):
        pltpu.make_async_copy(h_hbm.at[pl.ds(0, tm), :], xbuf.at[slot_], sem.at[slot_]).wait()

    @pl.when(step == 0)
    def _():
        for blk in range(ahead):
            def issue(r, c, blk=blk):
                row_copy(jnp.minimum(blk, last), blk, r).start()
                return c
            lax.fori_loop(0, tm, issue, 0, unroll=8)

    slot = step % FFN_BUFFERS
    wait_block(slot)
    xe = xbuf[slot].astype(BF16)
    nxt = jnp.minimum(step + ahead, last)
    nxt_slot = (step + ahead) % FFN_BUFFERS
    for r in range(tm):
        row_copy(nxt, nxt_slot, r).start()
    gate = jnp.dot(xe, wg_ref[0], preferred_element_type=F32)
    up = jnp.dot(xe, wu_ref[0], preferred_element_type=F32)
    hid = (gate * jax.nn.sigmoid(gate) * up).astype(BF16)
    ye = jnp.dot(hid, wd_ref[0], preferred_element_type=F32)
    slot_gate = jnp.concatenate([gate_ref[0]] * (ye.shape[1] // LANES), axis=1)
    o_ref[0] = (ye * slot_gate).astype(o_ref.dtype)

    @pl.when(step == last)
    def _():
        for k in range(1, FFN_BUFFERS):
            wait_block((step + k) % FFN_BUFFERS)


def _expert_ffn(idx, slot_gate, h, w_gate, w_up, w_down, tm=256):
    e, cap, _ = slot_gate.shape
    d = h.shape[1]
    ff = w_gate.shape[2]
    nb = cap // tm
    grid_spec = pltpu.PrefetchScalarGridSpec(
        num_scalar_prefetch=1,
        grid=(e, nb),
        in_specs=[pl.BlockSpec(memory_space=pl.ANY),
                  pl.BlockSpec((1, tm, LANES), lambda i, j, s: (i, j, 0)),
                  pl.BlockSpec((1, d, ff), lambda i, j, s: (i, 0, 0)),
                  pl.BlockSpec((1, d, ff), lambda i, j, s: (i, 0, 0)),
                  pl.BlockSpec((1, ff, d), lambda i, j, s: (i, 0, 0))],
        out_specs=pl.BlockSpec((1, tm, d), lambda i, j, s: (i, j, 0)),
        scratch_shapes=[pltpu.VMEM((FFN_BUFFERS, tm, d), F32), pltpu.SemaphoreType.DMA((FFN_BUFFERS,))],
    )
    return pl.pallas_call(
        functools.partial(_ffn_kernel, tm=tm),
        grid_spec=grid_spec,
        out_shape=jax.ShapeDtypeStruct((e, cap, d), BF16),
        name="expert_ffn",
        compiler_params=_cparams("arbitrary", "arbitrary"),
    )(idx, h, slot_gate, w_gate, w_up, w_down)


COMBINE_WINDOW = LANES // 2
SLOT_ALIGN = 16


def _combine_kernel(starts_ref, pos_ref, x_ref, ye_hbm, o_ref, ycat, yextra, sem, sem_extra, *, tm, cap, nt):
    i = pl.program_id(0)
    n_exp = pos_ref.shape[1]
    w = COMBINE_WINDOW
    slot = i % 2

    def base_of(e, tile):
        start = starts_ref[e * (nt + 1) + tile]
        return jnp.minimum((start // SLOT_ALIGN) * SLOT_ALIGN, cap - w)

    def window_copy(e, first_row, dst, dst_sem):
        src = ye_hbm.at[e, pl.ds(pl.multiple_of(first_row, SLOT_ALIGN), w), :]
        return pltpu.make_async_copy(src, dst.at[pl.ds(e * w, w), :], dst_sem)

    def prefetch(tile, slot_):
        for e in range(n_exp):
            window_copy(e, base_of(e, tile), ycat.at[slot_], sem.at[slot_]).start()

    @pl.when(i == 0)
    def _():
        prefetch(0, 0)

    @pl.when(i + 1 < nt)
    def _():
        prefetch(i + 1, 1 - slot)

    bases = [base_of(e, i) for e in range(n_exp)]
    lane = lax.broadcasted_iota(I32, (tm, LANES), 1)
    low_half = lane < w

    def onehot(first_rows, floors):
        cols = []
        for e in range(0, n_exp, 2):
            pos = jnp.where(low_half, pos_ref[:, e:e + 1], pos_ref[:, e + 1:e + 2])
            hit = lane == pos - jnp.where(low_half, first_rows[e], first_rows[e + 1] - w)
            if floors is not None:
                hit = hit & (pos >= jnp.where(low_half, floors[e], floors[e + 1]))
            cols.append(hit.astype(BF16))
        return jnp.concatenate(cols, axis=1)

    sel = onehot(bases, None)
    for e in range(n_exp):
        window_copy(e, bases[e], ycat.at[slot], sem.at[slot]).wait()
    o_ref[...] = x_ref[...] + jnp.dot(sel, ycat[slot], preferred_element_type=F32)

    npass = jnp.int32(0)
    for e in range(n_exp):
        end = starts_ref[e * (nt + 1) + i + 1]
        npass = jnp.maximum(npass, (end - bases[e] + w - 1) // w)

    def extra_pass(j, c):
        floors = [bases[e] + j * w for e in range(n_exp)]
        first_rows = [jnp.minimum(f, cap - w) for f in floors]
        for e in range(n_exp):
            window_copy(e, first_rows[e], yextra, sem_extra).start()
        for e in range(n_exp):
            window_copy(e, first_rows[e], yextra, sem_extra).wait()
        o_ref[...] += jnp.dot(onehot(first_rows, floors), yextra[...], preferred_element_type=F32)
        return c

    lax.fori_loop(1, npass, extra_pass, 0)


def _combine(starts, pos_t, x, ye, tm=256):
    n, d = x.shape
    e, cap, _ = ye.shape
    nt = n // tm
    grid_spec = pltpu.PrefetchScalarGridSpec(
        num_scalar_prefetch=1,
        grid=(nt,),
        in_specs=[pl.BlockSpec((tm, e), lambda i, s: (i, 0)),
                  pl.BlockSpec((tm, d), lambda i, s: (i, 0)),
                  pl.BlockSpec(memory_space=pl.ANY)],
        out_specs=pl.BlockSpec((tm, d), lambda i, s: (i, 0)),
        scratch_shapes=[pltpu.VMEM((2, e * COMBINE_WINDOW, d), BF16), pltpu.VMEM((e * COMBINE_WINDOW, d), BF16),
                        pltpu.SemaphoreType.DMA((2,)), pltpu.SemaphoreType.DMA(())],
    )
    return pl.pallas_call(
        functools.partial(_combine_kernel, tm=tm, cap=cap, nt=nt),
        grid_spec=grid_spec,
        out_shape=jax.ShapeDtypeStruct((n, d), F32),
        name="expert_combine",
        compiler_params=_cparams("arbitrary"),
    )(starts.reshape(-1), pos_t, x, ye)


def _expert_choice_ffn(x, g_ffn, w_router_t, w_gate, w_up, w_down, tm=256):
    n, _ = x.shape
    e = w_router_t.shape[0]
    cap = EC_CAPACITY * n // e
    h, aff = _norm_router(x, g_ffn, w_router_t)
    pos, idx, slot_gate, off = _route(aff.reshape(e, n // LANES, LANES), cap)
    rows_per_tile = tm // LANES
    starts = jnp.concatenate([off[:, ::rows_per_tile, 0], jnp.full((e, 1), cap, I32)], axis=1)
    ye = _expert_ffn(idx.reshape(e * cap), slot_gate, h, w_gate, w_up, w_down, tm=tm)
    return _combine(starts, pos.reshape(e, n).T, x, ye, tm=tm)


def _prepare_layer(l, t, g_mix, w_in, pool_w, pool_scale, mla_gq, mla_gkv, mla_w_uq, mla_w_ukv, mla_q_norm,
                   mla_k_norm, na_q_norm, na_k_norm, na_rel_bias, b_gate, w_br_pool, w_br_mla, w_br_na, w_out,
                   g_ffn, w_router, w_e_gate, w_e_up, w_e_down):
    d = w_in.shape[1]
    off_cq = POOL_WIDTH
    off_na = off_cq + 2 * MLA_RANK + MLA_ROPE
    off_gate = off_na + 3 * NA_WIDTH
    wl = w_in[l]
    p = {}
    p["g_mix"] = g_mix[l]
    p["w_pool_in"] = wl[:, :off_cq].astype(BF16)
    p["w_mla_in"] = jnp.pad(wl[:, off_cq:off_na], ((0, 0), (0, MLA_SEG - (off_na - off_cq)))).astype(BF16)
    p["w_na_in"] = wl[:, off_na:off_gate].astype(BF16)
    p["w_gate_in"] = wl[:, off_gate:].astype(BF16)
    p["b_gate"] = b_gate[l].reshape(N_BRANCH * d)
    p["pool_w"] = pool_w[l].astype(BF16)
    p["pool_scale"] = pool_scale[l]
    p["gq"] = mla_gq[l].reshape(1, MLA_RANK)
    p["gkv"] = mla_gkv[l].reshape(1, MLA_RANK)
    wq = mla_w_uq[l].reshape(MLA_RANK, MLA_HEADS, MLA_QK)
    wq_rope = jnp.pad(wq[:, :, MLA_NOPE:], ((0, 0), (0, 0), (0, LANES - MLA_ROPE)))
    p["wuq"] = jnp.concatenate([wq[:, :, :MLA_NOPE].reshape(MLA_RANK, -1), wq_rope.reshape(MLA_RANK, -1)],
                               axis=1).astype(BF16)
    wkv = mla_w_ukv[l].reshape(MLA_RANK, MLA_HEADS, MLA_NOPE + MLA_V)
    p["wuk"] = wkv[:, :, :MLA_NOPE].reshape(MLA_RANK, -1).astype(BF16)
    p["wvt"] = wkv[:, :, MLA_NOPE:].reshape(MLA_RANK, -1).T.astype(BF16)
    pad_norm = lambda g: jnp.pad(g, (0, 2 * LANES - MLA_QK)).reshape(1, 2 * LANES)
    p["qn"] = pad_norm(mla_q_norm[l] * (MLA_QK ** -0.5 * LOG2E))
    p["kn"] = pad_norm(mla_k_norm[l])
    p["na_qn"] = jnp.tile(na_q_norm[l] * (NA_HEAD_DIM ** -0.5 * LOG2E), 2).reshape(1, LANES)
    p["na_kn"] = jnp.tile(na_k_norm[l], 2).reshape(1, LANES)
    p["na_bias"] = _na_bias_tables(na_rel_bias[l], t // GRID_W)
    p["w_br_pool"] = w_br_pool[l].astype(BF16)
    p["w_br_mla"] = w_br_mla[l].astype(BF16)
    p["w_br_na"] = w_br_na[l].astype(BF16)
    p["w_out"] = w_out[l].astype(BF16)
    p["g_ffn"] = g_ffn[l]
    p["w_router_t"] = w_router[l].T
    p["w_e_gate"] = w_e_gate[l].astype(BF16)
    p["w_e_up"] = w_e_up[l].astype(BF16)
    p["w_e_down"] = w_e_down[l].astype(BF16)
    return p


def _rope_tables(t):
    half = MLA_ROPE // 2
    pos = jnp.arange(t, dtype=F32)
    inv = 1.0 / (ROPE_THETA ** (jnp.arange(0, MLA_ROPE, 2, dtype=F32) / MLA_ROPE))
    ang = pos[:, None] * inv[None, :]
    cos = jnp.cos(ang)
    sin = jnp.sin(ang)
    zeros = jnp.zeros((t, LANES - MLA_ROPE), F32)
    return jnp.concatenate([cos, cos, zeros], axis=1), jnp.concatenate([-sin, sin, zeros], axis=1)


def _layer(x, p, cos_t, sin_t, b, t):
    m, d = x.shape
    h = _rmsnorm(x, p["g_mix"])
    u = _matmul(h, p["w_pool_in"], F32, tn=POOL_WIDTH, name="in_proj_pool")
    mla_seg = _matmul(h, p["w_mla_in"], F32, tn=MLA_SEG, name="in_proj_mla")
    na_qkv = _matmul(h, p["w_na_in"], F32, tn=1024, name="in_proj_na")
    gates = _matmul(h, p["w_gate_in"], BF16, tn=1024, name="in_proj_gates", bias=p["b_gate"])
    a_pool = _pool_mixer(u.reshape(b, t, POOL_WIDTH), p["pool_w"], p["pool_scale"]).reshape(m, POOL_WIDTH)
    q, k, vt = _mla_prep(mla_seg, p["gq"], p["gkv"], p["wuq"], p["wuk"], p["wvt"], p["qn"], p["kn"], cos_t, sin_t,
                         b, t)
    a_mla = _mla_attn(q, k, vt).reshape(m, MLA_HEADS * MLA_V)
    a_na = _na_attn(na_qkv.reshape(b, t, 3 * NA_WIDTH), p["na_qn"], p["na_kn"], p["na_bias"], b, t).reshape(m, NA_WIDTH)
    merged = _merge(a_pool, a_mla, a_na, p["w_br_pool"], p["w_br_mla"], p["w_br_na"], gates)
    x = _matmul(merged, p["w_out"], F32, tn=1024, name="out_proj", residual=x)
    return _expert_choice_ffn(x, p["g_ffn"], p["w_router_t"], p["w_e_gate"], p["w_e_up"], p["w_e_down"])


def _trunk(x, layers, cos_t, sin_t):
    b, t, d = x.shape
    x = x.reshape(b * t, d)
    for p in layers:
        x = _layer(x, p, cos_t, sin_t, b, t)
    return x.reshape(b, t, d)


def kernel(x_prompt, x_sample, g_mix, w_in, pool_w, pool_scale, mla_gq, mla_gkv, mla_w_uq, mla_w_ukv, mla_q_norm,
           mla_k_norm, na_q_norm, na_k_norm, na_rel_bias, b_gate, w_br_pool, w_br_mla, w_br_na, w_out, g_ffn,
           w_router, w_e_gate, w_e_up, w_e_down):
    weights = (g_mix, w_in, pool_w, pool_scale, mla_gq, mla_gkv, mla_w_uq, mla_w_ukv, mla_q_norm, mla_k_norm,
               na_q_norm, na_k_norm, na_rel_bias, b_gate, w_br_pool, w_br_mla, w_br_na, w_out, g_ffn, w_router,
               w_e_gate, w_e_up, w_e_down)
    prepared = {}
    outs = []
    for x in (x_prompt, x_sample):
        t = x.shape[1]
        if t not in prepared:
            prepared[t] = ([_prepare_layer(l, t, *weights) for l in range(g_mix.shape[0])], _rope_tables(t))
        layers, (cos_t, sin_t) = prepared[t]
        outs.append(_trunk(x, layers, cos_t, sin_t))
    return tuple(outs)
```
